```python
import math
import jax, jax.numpy as jnp
from jax import lax
import numpy as np

D_MODEL = 1024
BATCH = 2
SEQ = 16384
DEPTH = 2

MIX_WIDTH = D_MODEL
A_WIDTH = MIX_WIDTH // 2
B_WIDTH = MIX_WIDTH - A_WIDTH
C_WIDTH = MIX_WIDTH // 2
D_WIDTH = MIX_WIDTH - C_WIDTH
EVEN_IN_WIDTH = A_WIDTH + 3 * B_WIDTH
ODD_IN_WIDTH = 2 * C_WIDTH + 3 * D_WIDTH
S5_GROUP = 16
S5_GROUPS = A_WIDTH // S5_GROUP
S5_STATE = 64
DT_MIN = 1e-3
DT_MAX = 1e-1
CONV_WIDTH = 3
GMLP_CHUNK = 128
GMLP_HEADS = 8
GMLP_HEAD_DIM = C_WIDTH // GMLP_HEADS
SB_HEADS = 8
SB_HEAD_DIM = D_WIDTH // SB_HEADS
SB_BLOCK = 128
PEER_HEADS = 8
PEER_N_KEYS = 128
PEER_N_EXPERTS = PEER_N_KEYS ** 2
PEER_TOPK = 16
PEER_QUERY_DIM = 256
PEER_HALF = PEER_QUERY_DIM // 2
PEER_TOKEN_BLOCK = 128
RMS_EPS = 1e-6

kernel_name = "hybrid_s5_shortconv_gmlp_stickbreak_peer"


def rmsnorm(x, g):
    xf = x.astype(jnp.float32)
    y = xf * lax.rsqrt(jnp.mean(xf * xf, axis=-1, keepdims=True) + RMS_EPS)
    return (y * g.astype(jnp.float32)).astype(x.dtype)


def _complex_affine_combine(e1, e2):
    a1r, a1i, b1r, b1i = e1
    a2r, a2i, b2r, b2i = e2
    ar = a2r * a1r - a2i * a1i
    ai = a2r * a1i + a2i * a1r
    br = a2r * b1r - a2i * b1i + b2r
    bi = a2r * b1i + a2i * b1r + b2i
    return (ar, ai, br, bi)


def s5_mixer(u, lam_re, lam_im, log_dt, b_re, b_im, c_re, c_im, d_skip, glu_w, glu_b):
    bsz, seqlen, _ = u.shape
    dt = jnp.exp(log_dt.astype(jnp.float32))[:, None]
    lr = lam_re.astype(jnp.float32)
    li = lam_im.astype(jnp.float32)
    mag = jnp.exp(lr * dt)
    ar = mag * jnp.cos(li * dt)
    ai = mag * jnp.sin(li * dt)
    den = lr * lr + li * li
    nr = ar - 1.0
    fr = (nr * lr + ai * li) / den
    fi = (ai * lr - nr * li) / den
    bre = b_re.astype(jnp.float32)
    bim = b_im.astype(jnp.float32)
    bbar_re = fr[..., None] * bre - fi[..., None] * bim
    bbar_im = fr[..., None] * bim + fi[..., None] * bre
    uf = u.astype(jnp.float32).reshape(bsz, seqlen, S5_GROUPS, S5_GROUP)
    bu_re = jnp.einsum('gpc,blgc->blgp', bbar_re, uf)
    bu_im = jnp.einsum('gpc,blgc->blgp', bbar_im, uf)
    a_re = jnp.broadcast_to(ar, bu_re.shape)
    a_im = jnp.broadcast_to(ai, bu_im.shape)
    _, _, h_re, h_im = lax.associative_scan(_complex_affine_combine, (a_re, a_im, bu_re, bu_im), axis=1)
    y = (jnp.einsum('gcp,blgp->blgc', c_re.astype(jnp.float32), h_re)
         - jnp.einsum('gcp,blgp->blgc', c_im.astype(jnp.float32), h_im)
         + d_skip.astype(jnp.float32) * uf)
    y = jax.nn.gelu(y.reshape(bsz, seqlen, A_WIDTH))
    out = y * jax.nn.sigmoid(y @ glu_w.astype(jnp.float32) + glu_b.astype(jnp.float32))
    return out.astype(u.dtype)


def short_conv_mixer(b_gate, c_gate, x_in, conv_w):
    z = c_gate * x_in
    w = conv_w.astype(z.dtype)[:, None, :]
    conv = lax.conv_general_dilated(z, w, window_strides=(1,), padding=[(CONV_WIDTH - 1, 0)],
                                    dimension_numbers=('NWC', 'WIO', 'NWC'),
                                    feature_group_count=B_WIDTH)
    return b_gate * conv


def gmlp_mixer(u, v, vnorm_g, ws, bs):
    bsz, seqlen, _ = v.shape
    u = jax.nn.gelu(u)
    v = rmsnorm(jax.nn.gelu(v), vnorm_g)
    vc = v.reshape(bsz, seqlen // GMLP_CHUNK, GMLP_CHUNK, GMLP_HEADS, GMLP_HEAD_DIM)
    tril = jnp.tril(jnp.ones((GMLP_CHUNK, GMLP_CHUNK), dtype=bool))
    wm = jnp.where(tril, ws, jnp.zeros_like(ws))
    y = jnp.einsum('hts,bnshd->bnthd', wm, vc) + bs.T[None, None, :, :, None]
    return u * y.reshape(bsz, seqlen, C_WIDTH)


def stick_breaking_attention(q, k, v):
    bsz, seqlen, _ = q.shape
    nblk = seqlen // SB_BLOCK
    qf = q.astype(jnp.float32).reshape(bsz, seqlen, SB_HEADS, SB_HEAD_DIM) * (SB_HEAD_DIM ** -0.5)
    qb = qf.reshape(bsz, nblk, SB_BLOCK, SB_HEADS, SB_HEAD_DIM).transpose(1, 0, 3, 2, 4)
    kf = k.astype(jnp.float32).reshape(bsz, seqlen, SB_HEADS, SB_HEAD_DIM)
    vf = v.astype(jnp.float32).reshape(bsz, seqlen, SB_HEADS, SB_HEAD_DIM)
    key_pos = jnp.arange(seqlen)

    def block(args):
        q_blk, bidx = args
        q_pos = bidx * SB_BLOCK + jnp.arange(SB_BLOCK)
        z = jnp.einsum('bhqd,bkhd->bhqk', q_blk, kf)
        mask = key_pos[None, :] < q_pos[:, None]
        log_1mb = jnp.where(mask, jax.nn.log_sigmoid(-z), 0.0)
        rest = lax.cumsum(log_1mb, axis=3, reverse=True) - log_1mb
        w = jnp.where(mask, jnp.exp(jax.nn.log_sigmoid(z) + rest), 0.0)
        return jnp.einsum('bhqk,bkhd->bqhd', w, vf)

    o = lax.map(block, (qb, jnp.arange(nblk)))
    o = o.transpose(1, 0, 2, 3, 4).reshape(bsz, seqlen, D_WIDTH)
    return o.astype(q.dtype)


def peer_ffn(h, wq, k1, k2, u_tab, v_tab):
    bsz, seqlen, dm = h.shape
    hb = h.reshape(-1, PEER_TOKEN_BLOCK, dm)
    k1f = k1.astype(jnp.float32)
    k2f = k2.astype(jnp.float32)

    def block(hx):
        q = (hx @ wq).astype(jnp.float32).reshape(PEER_TOKEN_BLOCK, PEER_HEADS, 2, PEER_HALF)
        s1 = jnp.einsum('thd,hnd->thn', q[:, :, 0], k1f)
        s2 = jnp.einsum('thd,hnd->thn', q[:, :, 1], k2f)
        v1, i1 = lax.top_k(s1, PEER_TOPK)
        v2, i2 = lax.top_k(s2, PEER_TOPK)
        cand_s = (v1[..., :, None] + v2[..., None, :]).reshape(PEER_TOKEN_BLOCK, PEER_HEADS, PEER_TOPK * PEER_TOPK)
        cand_i = (i1[..., :, None] * PEER_N_KEYS + i2[..., None, :]).reshape(PEER_TOKEN_BLOCK, PEER_HEADS, PEER_TOPK * PEER_TOPK)
        top_s, pos = lax.top_k(cand_s, PEER_TOPK)
        idx = jnp.take_along_axis(cand_i, pos, axis=-1)
        g = jax.nn.softmax(top_s, axis=-1)
        u_sel = u_tab[idx]
        act = jax.nn.gelu(jnp.einsum('thkd,td->thk', u_sel, hx).astype(jnp.float32))
        v_sel = v_tab[idx]
        return jnp.einsum('thk,thkd->td', (g * act).astype(v_sel.dtype), v_sel).astype(hx.dtype)

    return lax.map(block, hb).reshape(bsz, seqlen, dm)


def even_mixer(h, w_in, lam_re, lam_im, log_dt, b_re, b_im, c_re, c_im, d_skip, glu_w, glu_b, conv_w, w_out):
    p = h @ w_in
    a_u = p[..., :A_WIDTH]
    b_b = p[..., A_WIDTH:A_WIDTH + B_WIDTH]
    b_c = p[..., A_WIDTH + B_WIDTH:A_WIDTH + 2 * B_WIDTH]
    b_x = p[..., A_WIDTH + 2 * B_WIDTH:]
    ya = s5_mixer(a_u, lam_re, lam_im, log_dt, b_re, b_im, c_re, c_im, d_skip, glu_w, glu_b)
    yb = short_conv_mixer(b_b, b_c, b_x, conv_w)
    return jnp.concatenate([ya, yb], axis=-1) @ w_out


def odd_mixer(h, w_in, vnorm_g, ws, bs, w_out):
    p = h @ w_in
    c_u = p[..., :C_WIDTH]
    c_v = p[..., C_WIDTH:2 * C_WIDTH]
    o = 2 * C_WIDTH
    d_q = p[..., o:o + D_WIDTH]
    d_k = p[..., o + D_WIDTH:o + 2 * D_WIDTH]
    d_v = p[..., o + 2 * D_WIDTH:]
    yc = gmlp_mixer(c_u, c_v, vnorm_g, ws, bs)
    yd = stick_breaking_attention(d_q, d_k, d_v)
    return jnp.concatenate([yc, yd], axis=-1) @ w_out


def setup_inputs(seed: int = 0) -> dict:
    key = jax.random.key(seed)
    ks = iter(jax.random.split(key, 48))

    def nrm(shape, scale):
        return jax.random.normal(next(ks), shape, jnp.float32) * scale

    def gain(n):
        return 1.0 + nrm((n,), 0.02)

    def peer_params():
        return (nrm((D_MODEL, PEER_HEADS * PEER_QUERY_DIM), D_MODEL ** -0.5),
                nrm((PEER_HEADS, PEER_N_KEYS, PEER_HALF), PEER_HALF ** -0.5),
                nrm((PEER_HEADS, PEER_N_KEYS, PEER_HALF), PEER_HALF ** -0.5),
                nrm((PEER_N_EXPERTS, D_MODEL), D_MODEL ** -0.5),
                nrm((PEER_N_EXPERTS, D_MODEL), PEER_HEADS ** -0.5))

    x = nrm((BATCH, SEQ, D_MODEL), 1.0)
    l0_norm_mix_g = gain(D_MODEL)
    l0_w_in = nrm((D_MODEL, EVEN_IN_WIDTH), D_MODEL ** -0.5)
    a_lam_re = -0.5 + nrm((S5_GROUPS, S5_STATE), 0.01)
    a_lam_im = math.pi * jnp.arange(S5_STATE, dtype=jnp.float32)[None, :] + nrm((S5_GROUPS, S5_STATE), 0.01)
    a_log_dt = jax.random.uniform(next(ks), (S5_GROUPS,), jnp.float32, math.log(DT_MIN), math.log(DT_MAX))
    a_b_re = nrm((S5_GROUPS, S5_STATE, S5_GROUP), (2 * S5_GROUP) ** -0.5)
    a_b_im = nrm((S5_GROUPS, S5_STATE, S5_GROUP), (2 * S5_GROUP) ** -0.5)
    a_c_re = nrm((S5_GROUPS, S5_GROUP, S5_STATE), S5_STATE ** -0.5)
    a_c_im = nrm((S5_GROUPS, S5_GROUP, S5_STATE), S5_STATE ** -0.5)
    a_d = nrm((S5_GROUPS, S5_GROUP), 1.0)
    a_glu_w = nrm((A_WIDTH, A_WIDTH), A_WIDTH ** -0.5)
    a_glu_b = nrm((A_WIDTH,), 0.02)
    b_conv_w = nrm((CONV_WIDTH, B_WIDTH), CONV_WIDTH ** -0.5)
    l0_w_out = nrm((MIX_WIDTH, D_MODEL), MIX_WIDTH ** -0.5)
    l0_norm_ffn_g = gain(D_MODEL)
    l0_peer_wq, l0_peer_k1, l0_peer_k2, l0_peer_u, l0_peer_v = peer_params()
    l1_norm_mix_g = gain(D_MODEL)
    l1_w_in = nrm((D_MODEL, ODD_IN_WIDTH), D_MODEL ** -0.5)
    c_vnorm_g = gain(C_WIDTH)
    c_ws = nrm((GMLP_HEADS, GMLP_CHUNK, GMLP_CHUNK), 0.5 * GMLP_CHUNK ** -0.5)
    c_bs = 1.0 + nrm((GMLP_HEADS, GMLP_CHUNK), 0.1)
    l1_w_out = nrm((MIX_WIDTH, D_MODEL), MIX_WIDTH ** -0.5)
    l1_norm_ffn_g = gain(D_MODEL)
    l1_peer_wq, l1_peer_k1, l1_peer_k2, l1_peer_u, l1_peer_v = peer_params()
    final_norm_g = gain(D_MODEL)
    return {
        "x": x,
        "l0_norm_mix_g": l0_norm_mix_g, "l0_w_in": l0_w_in,
        "a_lam_re": a_lam_re, "a_lam_im": a_lam_im, "a_log_dt": a_log_dt,
        "a_b_re": a_b_re, "a_b_im": a_b_im, "a_c_re": a_c_re, "a_c_im": a_c_im, "a_d": a_d,
        "a_glu_w": a_glu_w, "a_glu_b": a_glu_b, "b_conv_w": b_conv_w, "l0_w_out": l0_w_out,
        "l0_norm_ffn_g": l0_norm_ffn_g, "l0_peer_wq": l0_peer_wq, "l0_peer_k1": l0_peer_k1,
        "l0_peer_k2": l0_peer_k2, "l0_peer_u": l0_peer_u, "l0_peer_v": l0_peer_v,
        "l1_norm_mix_g": l1_norm_mix_g, "l1_w_in": l1_w_in, "c_vnorm_g": c_vnorm_g,
        "c_ws": c_ws, "c_bs": c_bs, "l1_w_out": l1_w_out,
        "l1_norm_ffn_g": l1_norm_ffn_g, "l1_peer_wq": l1_peer_wq, "l1_peer_k1": l1_peer_k1,
        "l1_peer_k2": l1_peer_k2, "l1_peer_u": l1_peer_u, "l1_peer_v": l1_peer_v,
        "final_norm_g": final_norm_g,
    }


def reference(x, l0_norm_mix_g, l0_w_in, a_lam_re, a_lam_im, a_log_dt, a_b_re, a_b_im, a_c_re, a_c_im, a_d,
              a_glu_w, a_glu_b, b_conv_w, l0_w_out, l0_norm_ffn_g, l0_peer_wq, l0_peer_k1, l0_peer_k2,
              l0_peer_u, l0_peer_v, l1_norm_mix_g, l1_w_in, c_vnorm_g, c_ws, c_bs, l1_w_out, l1_norm_ffn_g,
              l1_peer_wq, l1_peer_k1, l1_peer_k2, l1_peer_u, l1_peer_v, final_norm_g):
    layers = (
        (l0_norm_mix_g, (l0_w_in, a_lam_re, a_lam_im, a_log_dt, a_b_re, a_b_im, a_c_re, a_c_im, a_d,
                         a_glu_w, a_glu_b, b_conv_w, l0_w_out),
         l0_norm_ffn_g, (l0_peer_wq, l0_peer_k1, l0_peer_k2, l0_peer_u, l0_peer_v)),
        (l1_norm_mix_g, (l1_w_in, c_vnorm_g, c_ws, c_bs, l1_w_out),
         l1_norm_ffn_g, (l1_peer_wq, l1_peer_k1, l1_peer_k2, l1_peer_u, l1_peer_v)),
    )
    h = x
    for layer in range(DEPTH):
        norm_mix_g, mix_params, norm_ffn_g, peer_params = layers[layer]
        hn = rmsnorm(h, norm_mix_g)
        if layer % 2 == 0:
            h = h + even_mixer(hn, *mix_params)
        else:
            h = h + odd_mixer(hn, *mix_params)
        h = h + peer_ffn(rmsnorm(h, norm_ffn_g), *peer_params)
    return rmsnorm(h, final_norm_g)
```

```python
import functools
import math

import jax
import jax.numpy as jnp
from jax import lax
from jax.experimental import pallas as pl
from jax.experimental.pallas import tpu as pltpu

F32 = jnp.float32
BF16 = jnp.bfloat16
NEG_INF = float("-inf")

RMS_EPS = 1e-6
LANES = 128
S5_GROUP = 16
S5_STATE = 64
S5_QUARTERS = 4
PEER_HEADS = 8
PEER_KEYS = 128
PEER_TOPK = 16
SB_HEAD_DIM = 64
GMLP_CHUNK = 128
GMLP_HEAD_DIM = 64
SB_LOG_CUTOFF = -104.0

TM_PROJ = 512
T_S5 = 256
T_GMLP = 512
TQ_SB = 128
TK_SB = 128
T_PEER = 512
E_BLK = 1024
VMEM_LIMIT = 48 * 1024 * 1024


def _cparams(sem):
    return pltpu.CompilerParams(dimension_semantics=sem, vmem_limit_bytes=VMEM_LIMIT)


def _dot(a, b):
    return jnp.dot(a, b, preferred_element_type=F32)


def _dot_nt(a, b):
    return lax.dot_general(a, b, (((1,), (1,)), ((), ())), preferred_element_type=F32)


def _gelu(x):
    c = math.sqrt(2.0 / math.pi)
    return 0.5 * x * (1.0 + jnp.tanh(c * (x + 0.044715 * (x * x * x))))


def _rms(x, g):
    return x * lax.rsqrt(jnp.mean(x * x, axis=-1, keepdims=True) + RMS_EPS) * g


def _norm_matmul_kernel(x_ref, g_ref, w_ref, *out_refs, splits):
    xn = _rms(x_ref[...], g_ref[...]).astype(BF16)
    p = _dot(xn, w_ref[...])
    for o_ref, (lo, hi, scale) in zip(out_refs, splits):
        part = p[:, lo:hi]
        if scale != 1.0:
            part = part * scale
        o_ref[...] = part.astype(o_ref.dtype)


def _norm_matmul(x, g, w, splits, dtypes):
    n, d = x.shape
    m = w.shape[1]
    tm = min(TM_PROJ, n)
    out_shape = [jax.ShapeDtypeStruct((n, hi - lo), dt) for (lo, hi, _), dt in zip(splits, dtypes)]
    out_specs = [pl.BlockSpec((tm, hi - lo), lambda i: (i, 0)) for (lo, hi, _) in splits]
    return pl.pallas_call(
        functools.partial(_norm_matmul_kernel, splits=splits),
        grid=(n // tm,),
        in_specs=[pl.BlockSpec((tm, d), lambda i: (i, 0)),
                  pl.BlockSpec((1, d), lambda i: (0, 0)),
                  pl.BlockSpec((d, m), lambda i: (0, 0))],
        out_specs=out_specs,
        out_shape=out_shape,
        compiler_params=_cparams(("parallel",)),
        name="norm_matmul",
    )(x, g.reshape(1, d), w)


def _even_mixer_kernel(p_ref, bq_ref, are_ref, aim_ref, cq_ref, d_ref, gw_ref, gb_ref, cw_ref,
                       ya_ref, yb_ref, bu_ref, hst_ref, zs_ref):
    t_len = p_ref.shape[0]
    aw = ya_ref.shape[1]
    qs = 2 * aw
    half = qs // 2

    @pl.when(pl.program_id(1) == 0)
    def _():
        hst_ref[...] = jnp.zeros_like(hst_ref)
        zs_ref[0:8, :] = jnp.zeros((8, zs_ref.shape[1]), F32)

    u = p_ref[:, 0:aw]
    ub = u.astype(BF16)
    for q in range(S5_QUARTERS):
        bu_ref[:, q * qs:(q + 1) * qs] = _dot(ub[:, q * LANES:(q + 1) * LANES], bq_ref[q])

    for q in range(S5_QUARTERS):
        ar = are_ref[q:q + 1, :]
        ai = aim_ref[q:q + 1, :]
        re_lo = q * qs
        im_lo = q * qs + half

        def step(t, carry, ar=ar, ai=ai, re_lo=re_lo, im_lo=im_lo):
            hr, hi = carry
            br = bu_ref[pl.ds(t, 1), re_lo:re_lo + half]
            bi = bu_ref[pl.ds(t, 1), im_lo:im_lo + half]
            nr = ar * hr - ai * hi + br
            ni = ar * hi + ai * hr + bi
            bu_ref[pl.ds(t, 1), re_lo:re_lo + half] = nr
            bu_ref[pl.ds(t, 1), im_lo:im_lo + half] = ni
            return nr, ni

        hr, hi = lax.fori_loop(0, t_len, step,
                               (hst_ref[2 * q:2 * q + 1, :], hst_ref[2 * q + 1:2 * q + 2, :]), unroll=8)
        hst_ref[2 * q:2 * q + 1, :] = hr
        hst_ref[2 * q + 1:2 * q + 2, :] = hi

    ys = [_dot(bu_ref[:, q * qs:(q + 1) * qs].astype(BF16), cq_ref[q]) for q in range(S5_QUARTERS)]
    y = jnp.concatenate(ys, axis=1) + d_ref[...] * u
    y = _gelu(y)
    gate = jax.nn.sigmoid(_dot(y.astype(BF16), gw_ref[...]) + gb_ref[...])
    ya_ref[...] = (y * gate).astype(ya_ref.dtype)

    bw = yb_ref.shape[1]
    z = p_ref[:, aw + bw:aw + 2 * bw] * p_ref[:, aw + 2 * bw:aw + 3 * bw]
    zs_ref[8:8 + t_len, :] = z
    conv = (cw_ref[2:3, :] * z + cw_ref[1:2, :] * zs_ref[7:7 + t_len, :]
            + cw_ref[0:1, :] * zs_ref[6:6 + t_len, :])
    yb_ref[...] = (p_ref[:, aw:aw + bw] * conv).astype(yb_ref.dtype)
    zs_ref[0:8, :] = zs_ref[t_len:t_len + 8, :]


def _s5_params(lam_re, lam_im, log_dt, b_re, b_im, c_re, c_im):
    g, p = lam_re.shape
    gq = g // S5_QUARTERS
    dt = jnp.exp(log_dt.astype(F32))[:, None]
    lr = lam_re.astype(F32)
    li = lam_im.astype(F32)
    mag = jnp.exp(lr * dt)
    ar = mag * jnp.cos(li * dt)
    ai = mag * jnp.sin(li * dt)
    den = lr * lr + li * li
    nr = ar - 1.0
    fr = (nr * lr + ai * li) / den
    fi = (ai * lr - nr * li) / den
    bre = b_re.astype(F32)
    bim = b_im.astype(F32)
    bbar_re = fr[..., None] * bre - fi[..., None] * bim
    bbar_im = fr[..., None] * bim + fi[..., None] * bre
    eye = jnp.eye(gq, dtype=F32)

    def in_blockdiag(m):
        m = m.reshape(S5_QUARTERS, gq, p, S5_GROUP)
        return jnp.einsum('qgpc,gh->qgchp', m, eye).reshape(S5_QUARTERS, gq * S5_GROUP, gq * p)

    def out_blockdiag(m):
        m = m.reshape(S5_QUARTERS, gq, S5_GROUP, p)
        return jnp.einsum('qgcp,gh->qgphc', m, eye).reshape(S5_QUARTERS, gq * p, gq * S5_GROUP)

    bq = jnp.concatenate([in_blockdiag(bbar_re), in_blockdiag(bbar_im)], axis=2).astype(BF16)
    cq = jnp.concatenate([out_blockdiag(c_re.astype(F32)), -out_blockdiag(c_im.astype(F32))],
                         axis=1).astype(BF16)
    are = ar.reshape(S5_QUARTERS, gq * p)
    aim = ai.reshape(S5_QUARTERS, gq * p)
    return bq, are, aim, cq


def _even_mixer(p, batch, bq, are, aim, cq, d_skip, glu_w, glu_b, conv_w):
    n, width = p.shape
    seq = n // batch
    aw = d_skip.size
    bw = conv_w.shape[1]
    t = min(T_S5, seq)
    nt = seq // t
    qs = 2 * aw
    full = lambda shape: pl.BlockSpec(shape, lambda b, i: (0,) * len(shape))
    return pl.pallas_call(
        _even_mixer_kernel,
        grid=(batch, nt),
        in_specs=[pl.BlockSpec((t, width), lambda b, i: (b * nt + i, 0)),
                  full(bq.shape), full(are.shape), full(aim.shape), full(cq.shape),
                  full((1, aw)), full(glu_w.shape), full((1, aw)), full(conv_w.shape)],
        out_specs=[pl.BlockSpec((t, aw), lambda b, i: (b * nt + i, 0)),
                   pl.BlockSpec((t, bw), lambda b, i: (b * nt + i, 0))],
        out_shape=[jax.ShapeDtypeStruct((n, aw), BF16), jax.ShapeDtypeStruct((n, bw), BF16)],
        scratch_shapes=[pltpu.VMEM((t, S5_QUARTERS * qs), F32),
                        pltpu.VMEM((2 * S5_QUARTERS, aw), F32),
                        pltpu.VMEM((t + 8, bw), F32)],
        compiler_params=_cparams(("arbitrary", "arbitrary")),
        name="even_mixer",
    )(p, bq, are, aim, cq, d_skip.reshape(1, aw).astype(F32), glu_w.astype(BF16),
      glu_b.reshape(1, aw).astype(F32), conv_w.astype(F32))


def _gmlp_kernel(uv_ref, g_ref, wm_ref, bias_ref, y_ref):
    cw = y_ref.shape[1]
    t_len = y_ref.shape[0]
    u = _gelu(uv_ref[:, 0:cw])
    v = _rms(_gelu(uv_ref[:, cw:2 * cw]), g_ref[...]).astype(BF16)
    lane = lax.broadcasted_iota(jnp.int32, (GMLP_CHUNK, LANES), 1)
    first_head = lane < GMLP_HEAD_DIM
    for c in range(t_len // GMLP_CHUNK):
        r0 = c * GMLP_CHUNK
        tiles = []
        for j in range(cw // LANES):
            vc = v[r0:r0 + GMLP_CHUNK, j * LANES:(j + 1) * LANES]
            tiles.append(jnp.where(first_head, _dot(wm_ref[2 * j], vc), _dot(wm_ref[2 * j + 1], vc)))
        y = jnp.concatenate(tiles, axis=1) + bias_ref[...]
        y_ref[r0:r0 + GMLP_CHUNK, :] = (u[r0:r0 + GMLP_CHUNK, :] * y).astype(y_ref.dtype)


def _gmlp(uv, vnorm_g, ws, bs):
    n = uv.shape[0]
    cw = uv.shape[1] // 2
    heads = ws.shape[0]
    t = min(T_GMLP, n)
    tril = jnp.tril(jnp.ones((GMLP_CHUNK, GMLP_CHUNK), dtype=bool))
    wm = jnp.where(tril, ws, jnp.zeros_like(ws)).astype(BF16)
    bias = jnp.repeat(bs.T.astype(F32), cw // heads, axis=1)
    return pl.pallas_call(
        _gmlp_kernel,
        grid=(n // t,),
        in_specs=[pl.BlockSpec((t, 2 * cw), lambda i: (i, 0)),
                  pl.BlockSpec((1, cw), lambda i: (0, 0)),
                  pl.BlockSpec(wm.shape, lambda i: (0, 0, 0)),
                  pl.BlockSpec(bias.shape, lambda i: (0, 0))],
        out_specs=pl.BlockSpec((t, cw), lambda i: (i, 0)),
        out_shape=jax.ShapeDtypeStruct((n, cw), BF16),
        compiler_params=_cparams(("parallel",)),
        name="gmlp",
    )(uv, vnorm_g.reshape(1, cw).astype(F32), wm, bias)


def _sb_attn_kernel(q_ref, k_ref, v_ref, tri_ref, o_ref, acc_ref, run_ref):
    qi = pl.program_id(2)
    tq = q_ref.shape[0]
    q = q_ref[...]
    lane = lax.broadcasted_iota(jnp.int32, q.shape, 1)
    row = lax.broadcasted_iota(jnp.int32, (tq, TK_SB), 0)
    col = lax.broadcasted_iota(jnp.int32, (tq, TK_SB), 1)
    outs = []
    for head in range(2):
        in_head = (lane < SB_HEAD_DIM) if head == 0 else (lane >= SB_HEAD_DIM)
        qm = jnp.where(in_head, q, jnp.zeros_like(q))
        acc_ref[...] = jnp.zeros_like(acc_ref)
        run_ref[...] = jnp.zeros_like(run_ref)

        def cond(c):
            kb, live = c
            return jnp.logical_and(kb >= 0, live)

        def body(c, qm=qm):
            kb, _ = c
            k0 = pl.multiple_of(kb * TK_SB, TK_SB)
            kblk = k_ref[pl.ds(k0, TK_SB), :]
            vblk = v_ref[pl.ds(k0, TK_SB), :]
            z = _dot_nt(qm, kblk)
            sp = jnp.maximum(z, 0.0) + jnp.log(1.0 + jnp.exp(-jnp.abs(z)))
            valid = (k0 + col) < (qi * tq + row)
            lm = jnp.where(valid, -sp, 0.0)
            lm_hi = lm.astype(BF16)
            lm_lo = (lm - lm_hi.astype(F32)).astype(BF16)
            cs = _dot(lm_hi, tri_ref[...]) + _dot(lm_lo, tri_ref[...])
            run = run_ref[...]
            w = jnp.where(valid, jnp.exp(z - sp + cs[:, :TK_SB] + run), 0.0)
            acc_ref[...] += _dot(w.astype(BF16), vblk)
            run = run + cs[:, TK_SB:]
            run_ref[...] = run
            return kb - 1, jnp.max(run) > SB_LOG_CUTOFF

        lax.while_loop(cond, body, (qi * (tq // TK_SB), True))
        outs.append(acc_ref[...])
    o_ref[...] = jnp.where(lane < SB_HEAD_DIM, outs[0], outs[1]).astype(o_ref.dtype)


def _sb_attention(q, k, v, batch):
    n, dw = q.shape
    seq = n // batch
    tq = min(TQ_SB, seq)
    nq = seq // tq
    npair = dw // LANES
    s_idx = jnp.arange(TK_SB)[:, None]
    j_idx = jnp.arange(TK_SB)[None, :]
    tri = jnp.concatenate([(s_idx > j_idx), jnp.ones((TK_SB, TK_SB), bool)], axis=1).astype(BF16)
    return pl.pallas_call(
        _sb_attn_kernel,
        grid=(batch, npair, nq),
        in_specs=[pl.BlockSpec((tq, LANES), lambda b, h, i: (b * nq + i, h)),
                  pl.BlockSpec((seq, LANES), lambda b, h, i: (b, h)),
                  pl.BlockSpec((seq, LANES), lambda b, h, i: (b, h)),
                  pl.BlockSpec(tri.shape, lambda b, h, i: (0, 0))],
        out_specs=pl.BlockSpec((tq, LANES), lambda b, h, i: (b * nq + i, h)),
        out_shape=jax.ShapeDtypeStruct((n, dw), BF16),
        scratch_shapes=[pltpu.VMEM((tq, LANES), F32), pltpu.VMEM((tq, TK_SB), F32)],
        compiler_params=_cparams(("parallel", "parallel", "arbitrary")),
        name="sb_attn",
    )(q, k, v, tri)


def _top_values(s, n, rows_out):
    riota = lax.broadcasted_iota(jnp.int32, (rows_out, s.shape[1]), 0)
    vals = jnp.full((rows_out, s.shape[1]), NEG_INF, F32)
    cur = s
    for i in range(n):
        m = jnp.max(cur, axis=0, keepdims=True)
        vals = jnp.where(riota == i, m, vals)
        cur = jnp.where(cur == m, NEG_INF, cur)
    return vals


def _peer_select_kernel(x_ref, ya_ref, yb_ref, wo_ref, g_ref, wq_ref, k1_ref, k2_ref,
                        h_ref, hx_ref, th_ref, e1_ref, s2_ref, e2_ref, s1_ref):
    t_len = x_ref.shape[0]
    wa = ya_ref.shape[1]

    @pl.when(pl.program_id(1) == 0)
    def _():
        mix = _dot(ya_ref[...], wo_ref[0:wa, :]) + _dot(yb_ref[...], wo_ref[wa:, :])
        h = x_ref[...] + mix
        h_ref[...] = h
        hx_ref[...] = _rms(h, g_ref[...]).astype(BF16)

    q = _dot(hx_ref[...], wq_ref[...])
    half = q.shape[1] // 2
    s1_ref[...] = _dot_nt(k1_ref[0], q[:, :half].astype(BF16))
    s2_ref[0] = _dot_nt(k2_ref[0], q[:, half:].astype(BF16))

    n = PEER_TOPK + 1
    rows = 24
    i8 = lax.broadcasted_iota(jnp.int32, (8, LANES), 0)

    def chunk(c, carry):
        l0 = pl.multiple_of(c * LANES, LANES)
        s1 = s1_ref[:, pl.ds(l0, LANES)]
        s2 = s2_ref[0, :, pl.ds(l0, LANES)]
        v1 = _top_values(s1, n, rows)
        v2 = _top_values(s2, n, rows)
        cands = [v1[0:1] + v2, v1[1:2] + v2[0:8]]
        for a, nb in ((2, 5), (3, 4), (4, 3), (5, 2), (6, 2), (7, 2)):
            cands.append(jnp.where(i8 < nb, v1[a:a + 1] + v2[0:8], NEG_INF))
        cands.append(v1[8:24] + v2[0:1])
        cand = jnp.concatenate(cands, axis=0)
        top = _top_values(cand, n, rows)
        t16 = top[PEER_TOPK - 1:PEER_TOPK]
        t17 = top[PEER_TOPK:PEER_TOPK + 1]
        tau = jnp.where(t17 == NEG_INF, t16, 0.5 * (t16 + t17))
        m1 = v1[0:1]
        m2 = v2[0:1]
        zsum = jnp.sum(jnp.where(cand >= tau, jnp.exp(cand - (m1 + m2)), 0.0), axis=0, keepdims=True)
        th_ref[0, :, pl.ds(l0, LANES)] = tau - s1
        e1_ref[0, :, pl.ds(l0, LANES)] = jnp.exp(s1 - m1)
        e2_ref[0, :, pl.ds(l0, LANES)] = jnp.exp(s2 - m2) / zsum
        return carry

    lax.fori_loop(0, t_len // LANES, chunk, 0)


def _peer_select(x, ya, yb, w_out, g, wq, k1, k2):
    n, d = x.shape
    wa = ya.shape[1]
    heads, keys, half = k1.shape
    t = min(T_PEER, n)
    sel_shape = jax.ShapeDtypeStruct((heads, keys, n), F32)
    sel_spec = pl.BlockSpec((1, keys, t), lambda i, h: (h, 0, i))
    return pl.pallas_call(
        _peer_select_kernel,
        grid=(n // t, heads),
        in_specs=[pl.BlockSpec((t, d), lambda i, h: (i, 0)),
                  pl.BlockSpec((t, wa), lambda i, h: (i, 0)),
                  pl.BlockSpec((t, yb.shape[1]), lambda i, h: (i, 0)),
                  pl.BlockSpec(w_out.shape, lambda i, h: (0, 0)),
                  pl.BlockSpec((1, d), lambda i, h: (0, 0)),
                  pl.BlockSpec((d, 2 * half), lambda i, h: (0, h)),
                  pl.BlockSpec((1, keys, half), lambda i, h: (h, 0, 0)),
                  pl.BlockSpec((1, keys, half), lambda i, h: (h, 0, 0))],
        out_specs=[pl.BlockSpec((t, d), lambda i, h: (i, 0)),
                   pl.BlockSpec((t, d), lambda i, h: (i, 0)),
                   sel_spec, sel_spec, sel_spec, sel_spec],
        out_shape=[jax.ShapeDtypeStruct((n, d), F32), jax.ShapeDtypeStruct((n, d), BF16),
                   sel_shape, sel_shape, sel_shape, sel_shape],
        scratch_shapes=[pltpu.VMEM((keys, t), F32)],
        compiler_params=_cparams(("parallel", "arbitrary")),
        name="peer_select",
    )(x, ya, yb, w_out.astype(BF16), g.reshape(1, d).astype(F32), wq.astype(BF16),
      k1.astype(BF16), k2.astype(BF16))


def _peer_dense_kernel(hx_ref, th_ref, e1_ref, s2_ref, e2_ref, u_ref, vt_ref, res_ref, gf_ref,
                       out_ref, acc_ref, w_ref, *, final_norm):
    j = pl.program_id(1)
    heads, keys, _ = th_ref.shape
    n_i1 = u_ref.shape[0] // keys
    rc = 32

    @pl.when(j == 0)
    def _():
        acc_ref[...] = jnp.zeros_like(acc_ref)

    hx = hx_ref[...]

    def i1_body(ii, carry):
        i1 = j * n_i1 + ii
        r0 = pl.multiple_of(ii * keys, keys)
        act = _dot_nt(u_ref[pl.ds(r0, keys), :], hx)
        for r in range(0, keys, rc):
            g = jnp.zeros((rc, act.shape[1]), F32)
            for h in range(heads):
                th = th_ref[h, pl.ds(i1, 1), :]
                e1 = e1_ref[h, pl.ds(i1, 1), :]
                g = g + jnp.where(s2_ref[h, r:r + rc, :] >= th, e2_ref[h, r:r + rc, :], 0.0) * e1
            w_ref[pl.ds(r0 + r, rc), :] = (g * _gelu(act[r:r + rc, :])).astype(BF16)
        return carry

    lax.fori_loop(0, n_i1, i1_body, 0)
    acc_ref[...] += _dot(vt_ref[...], w_ref[...])

    @pl.when(j == pl.num_programs(1) - 1)
    def _():
        o = res_ref[...] + acc_ref[...].T
        if final_norm:
            o = _rms(o, gf_ref[...])
        out_ref[...] = o


def _peer_dense(hx, th, e1, s2, e2, u_tab, vt_tab, res, g_final, final_norm):
    n, d = res.shape
    heads, keys, _ = th.shape
    n_exp = u_tab.shape[0]
    t = min(T_PEER, n)
    e_blk = min(E_BLK, n_exp)
    sel_spec = pl.BlockSpec((heads, keys, t), lambda i, j: (0, 0, i))
    return pl.pallas_call(
        functools.partial(_peer_dense_kernel, final_norm=final_norm),
        grid=(n // t, n_exp // e_blk),
        in_specs=[pl.BlockSpec((t, d), lambda i, j: (i, 0)),
                  sel_spec, sel_spec, sel_spec, sel_spec,
                  pl.BlockSpec((e_blk, d), lambda i, j: (j, 0)),
                  pl.BlockSpec((d, e_blk), lambda i, j: (0, j)),
                  pl.BlockSpec((t, d), lambda i, j: (i, 0)),
                  pl.BlockSpec((1, d), lambda i, j: (0, 0))],
        out_specs=pl.BlockSpec((t, d), lambda i, j: (i, 0)),
        out_shape=jax.ShapeDtypeStruct((n, d), F32),
        scratch_shapes=[pltpu.VMEM((d, t), F32), pltpu.VMEM((e_blk, t), BF16)],
        compiler_params=_cparams(("parallel", "arbitrary")),
        name="peer_dense",
    )(hx, th, e1, s2, e2, u_tab, vt_tab, res, g_final.reshape(1, d).astype(F32))


def _peer_block(x, ya, yb, w_out, g_ffn, wq, k1, k2, u_tab, v_tab, g_final, final_norm):
    h, hx, th, e1, s2, e2 = _peer_select(x, ya, yb, w_out, g_ffn, wq, k1, k2)
    return _peer_dense(hx, th, e1, s2, e2, u_tab.astype(BF16), v_tab.astype(BF16).T, h, g_final,
                       final_norm)


def kernel(x, l0_norm_mix_g, l0_w_in, a_lam_re, a_lam_im, a_log_dt, a_b_re, a_b_im, a_c_re, a_c_im, a_d, a_glu_w, a_glu_b, b_conv_w, l0_w_out, l0_norm_ffn_g, l0_peer_wq, l0_peer_k1, l0_peer_k2, l0_peer_u, l0_peer_v, l1_norm_mix_g, l1_w_in, c_vnorm_g, c_ws, c_bs, l1_w_out, l1_norm_ffn_g, l1_peer_wq, l1_peer_k1, l1_peer_k2, l1_peer_u, l1_peer_v, final_norm_g):
    batch, seq, d = x.shape
    n = batch * seq
    h = x.reshape(n, d)

    (p0,) = _norm_matmul(h, l0_norm_mix_g.astype(F32), l0_w_in.astype(BF16),
                         ((0, l0_w_in.shape[1], 1.0),), (F32,))
    bq, are, aim, cq = _s5_params(a_lam_re, a_lam_im, a_log_dt, a_b_re, a_b_im, a_c_re, a_c_im)
    ya, yb = _even_mixer(p0, batch, bq, are, aim, cq, a_d, a_glu_w, a_glu_b, b_conv_w)
    h = _peer_block(h, ya, yb, l0_w_out, l0_norm_ffn_g, l0_peer_wq, l0_peer_k1, l0_peer_k2,
                    l0_peer_u, l0_peer_v, final_norm_g, False)

    cw = c_vnorm_g.shape[0]
    dw = (l1_w_in.shape[1] - 2 * cw) // 3
    o = 2 * cw
    splits = ((0, o, 1.0), (o, o + dw, SB_HEAD_DIM ** -0.5), (o + dw, o + 2 * dw, 1.0),
              (o + 2 * dw, o + 3 * dw, 1.0))
    uv, q, k, v = _norm_matmul(h, l1_norm_mix_g.astype(F32), l1_w_in.astype(BF16), splits,
                               (F32, BF16, BF16, BF16))
    yc = _gmlp(uv, c_vnorm_g, c_ws, c_bs)
    yd = _sb_attention(q, k, v, batch)
    h = _peer_block(h, yc, yd, l1_w_out, l1_norm_ffn_g, l1_peer_wq, l1_peer_k1, l1_peer_k2,
                    l1_peer_u, l1_peer_v, final_norm_g, True)
    return h.reshape(batch, seq, d)
```

```python
import functools
import math

import jax
import jax.numpy as jnp
from jax import lax
from jax.experimental import pallas as pl
from jax.experimental.pallas import tpu as pltpu

F32 = jnp.float32
BF16 = jnp.bfloat16
NEG_INF = float("-inf")

RMS_EPS = 1e-6
LANES = 128
S5_GROUP = 16
S5_STATE = 64
S5_QUARTERS = 4
PEER_HEADS = 8
PEER_KEYS = 128
PEER_TOPK = 16
SB_HEAD_DIM = 64
GMLP_CHUNK = 128
GMLP_HEAD_DIM = 64
SB_LOG_CUTOFF = -104.0

TM_PROJ = 512
T_S5 = 256
T_GMLP = 512
TQ_SB = 256
TK_SB = 128
SB_STATIC_BLOCKS = 3
T_PEER = 512
E_BLK = 1024
VMEM_LIMIT = 48 * 1024 * 1024


def _cparams(sem, flags=None):
    return pltpu.CompilerParams(dimension_semantics=sem, vmem_limit_bytes=VMEM_LIMIT, flags=flags)


def _dot(a, b):
    return jnp.dot(a, b, preferred_element_type=F32)


def _dot_nt(a, b):
    return lax.dot_general(a, b, (((1,), (1,)), ((), ())), preferred_element_type=F32)


def _gelu(x):
    c = math.sqrt(2.0 / math.pi)
    return 0.5 * x * (1.0 + jnp.tanh(c * (x + 0.044715 * (x * x * x))))


def _rms(x, g):
    return x * lax.rsqrt(jnp.mean(x * x, axis=-1, keepdims=True) + RMS_EPS) * g


def _norm_matmul_kernel(x_ref, g_ref, w_ref, *out_refs, splits):
    xn = _rms(x_ref[...], g_ref[...]).astype(BF16)
    p = _dot(xn, w_ref[...])
    for o_ref, (lo, hi, scale) in zip(out_refs, splits):
        part = p[:, lo:hi]
        if scale != 1.0:
            part = part * scale
        o_ref[...] = part.astype(o_ref.dtype)


def _norm_matmul(x, g, w, splits, dtypes):
    n, d = x.shape
    m = w.shape[1]
    tm = min(TM_PROJ, n)
    out_shape = [jax.ShapeDtypeStruct((n, hi - lo), dt) for (lo, hi, _), dt in zip(splits, dtypes)]
    out_specs = [pl.BlockSpec((tm, hi - lo), lambda i: (i, 0)) for (lo, hi, _) in splits]
    return pl.pallas_call(
        functools.partial(_norm_matmul_kernel, splits=splits),
        grid=(n // tm,),
        in_specs=[pl.BlockSpec((tm, d), lambda i: (i, 0)),
                  pl.BlockSpec((1, d), lambda i: (0, 0)),
                  pl.BlockSpec((d, m), lambda i: (0, 0))],
        out_specs=out_specs,
        out_shape=out_shape,
        compiler_params=_cparams(("parallel",)),
        name="norm_matmul",
    )(x, g.reshape(1, d), w)


def _even_mixer_kernel(p_ref, bq_ref, are_ref, aim_ref, cq_ref, d_ref, gw_ref, gb_ref, cw_ref,
                       ya_ref, yb_ref, bu_ref, hst_ref, zs_ref):
    t_len = p_ref.shape[0]
    aw = ya_ref.shape[1]
    qs = 2 * aw
    half = qs // 2

    @pl.when(pl.program_id(1) == 0)
    def _():
        hst_ref[...] = jnp.zeros_like(hst_ref)
        zs_ref[0:8, :] = jnp.zeros((8, zs_ref.shape[1]), F32)

    u = p_ref[:, 0:aw]
    ub = u.astype(BF16)
    for q in range(S5_QUARTERS):
        bu_ref[:, q * qs:(q + 1) * qs] = _dot(ub[:, q * LANES:(q + 1) * LANES], bq_ref[q])

    for q in range(S5_QUARTERS):
        ar = are_ref[q:q + 1, :]
        ai = aim_ref[q:q + 1, :]
        re_lo = q * qs
        im_lo = q * qs + half

        def step(t, carry, ar=ar, ai=ai, re_lo=re_lo, im_lo=im_lo):
            hr, hi = carry
            br = bu_ref[pl.ds(t, 1), re_lo:re_lo + half]
            bi = bu_ref[pl.ds(t, 1), im_lo:im_lo + half]
            nr = ar * hr - ai * hi + br
            ni = ar * hi + ai * hr + bi
            bu_ref[pl.ds(t, 1), re_lo:re_lo + half] = nr
            bu_ref[pl.ds(t, 1), im_lo:im_lo + half] = ni
            return nr, ni

        hr, hi = lax.fori_loop(0, t_len, step,
                               (hst_ref[2 * q:2 * q + 1, :], hst_ref[2 * q + 1:2 * q + 2, :]), unroll=8)
        hst_ref[2 * q:2 * q + 1, :] = hr
        hst_ref[2 * q + 1:2 * q + 2, :] = hi

    ys = [_dot(bu_ref[:, q * qs:(q + 1) * qs].astype(BF16), cq_ref[q]) for q in range(S5_QUARTERS)]
    y = jnp.concatenate(ys, axis=1) + d_ref[...] * u
    y = _gelu(y)
    gate = jax.nn.sigmoid(_dot(y.astype(BF16), gw_ref[...]) + gb_ref[...])
    ya_ref[...] = (y * gate).astype(ya_ref.dtype)

    bw = yb_ref.shape[1]
    z = p_ref[:, aw + bw:aw + 2 * bw] * p_ref[:, aw + 2 * bw:aw + 3 * bw]
    zs_ref[8:8 + t_len, :] = z
    conv = (cw_ref[2:3, :] * z + cw_ref[1:2, :] * zs_ref[7:7 + t_len, :]
            + cw_ref[0:1, :] * zs_ref[6:6 + t_len, :])
    yb_ref[...] = (p_ref[:, aw:aw + bw] * conv).astype(yb_ref.dtype)
    zs_ref[0:8, :] = zs_ref[t_len:t_len + 8, :]


def _s5_params(lam_re, lam_im, log_dt, b_re, b_im, c_re, c_im):
    g, p = lam_re.shape
    gq = g // S5_QUARTERS
    dt = jnp.exp(log_dt.astype(F32))[:, None]
    lr = lam_re.astype(F32)
    li = lam_im.astype(F32)
    mag = jnp.exp(lr * dt)
    ar = mag * jnp.cos(li * dt)
    ai = mag * jnp.sin(li * dt)
    den = lr * lr + li * li
    nr = ar - 1.0
    fr = (nr * lr + ai * li) / den
    fi = (ai * lr - nr * li) / den
    bre = b_re.astype(F32)
    bim = b_im.astype(F32)
    bbar_re = fr[..., None] * bre - fi[..., None] * bim
    bbar_im = fr[..., None] * bim + fi[..., None] * bre
    eye = jnp.eye(gq, dtype=F32)

    def in_blockdiag(m):
        m = m.reshape(S5_QUARTERS, gq, p, S5_GROUP)
        return jnp.einsum('qgpc,gh->qgchp', m, eye).reshape(S5_QUARTERS, gq * S5_GROUP, gq * p)

    def out_blockdiag(m):
        m = m.reshape(S5_QUARTERS, gq, S5_GROUP, p)
        return jnp.einsum('qgcp,gh->qgphc', m, eye).reshape(S5_QUARTERS, gq * p, gq * S5_GROUP)

    bq = jnp.concatenate([in_blockdiag(bbar_re), in_blockdiag(bbar_im)], axis=2).astype(BF16)
    cq = jnp.concatenate([out_blockdiag(c_re.astype(F32)), -out_blockdiag(c_im.astype(F32))],
                         axis=1).astype(BF16)
    are = ar.reshape(S5_QUARTERS, gq * p)
    aim = ai.reshape(S5_QUARTERS, gq * p)
    return bq, are, aim, cq


def _even_mixer(p, batch, bq, are, aim, cq, d_skip, glu_w, glu_b, conv_w):
    n, width = p.shape
    seq = n // batch
    aw = d_skip.size
    bw = conv_w.shape[1]
    t = min(T_S5, seq)
    nt = seq // t
    qs = 2 * aw
    full = lambda shape: pl.BlockSpec(shape, lambda b, i: (0,) * len(shape))
    return pl.pallas_call(
        _even_mixer_kernel,
        grid=(batch, nt),
        in_specs=[pl.BlockSpec((t, width), lambda b, i: (b * nt + i, 0)),
                  full(bq.shape), full(are.shape), full(aim.shape), full(cq.shape),
                  full((1, aw)), full(glu_w.shape), full((1, aw)), full(conv_w.shape)],
        out_specs=[pl.BlockSpec((t, aw), lambda b, i: (b * nt + i, 0)),
                   pl.BlockSpec((t, bw), lambda b, i: (b * nt + i, 0))],
        out_shape=[jax.ShapeDtypeStruct((n, aw), BF16), jax.ShapeDtypeStruct((n, bw), BF16)],
        scratch_shapes=[pltpu.VMEM((t, S5_QUARTERS * qs), F32),
                        pltpu.VMEM((2 * S5_QUARTERS, aw), F32),
                        pltpu.VMEM((t + 8, bw), F32)],
        compiler_params=_cparams(("arbitrary", "arbitrary")),
        name="even_mixer",
    )(p, bq, are, aim, cq, d_skip.reshape(1, aw).astype(F32), glu_w.astype(BF16),
      glu_b.reshape(1, aw).astype(F32), conv_w.astype(F32))


def _gmlp_kernel(uv_ref, g_ref, wm_ref, bias_ref, y_ref):
    cw = y_ref.shape[1]
    t_len = y_ref.shape[0]
    u = _gelu(uv_ref[:, 0:cw])
    v = _rms(_gelu(uv_ref[:, cw:2 * cw]), g_ref[...]).astype(BF16)
    lane = lax.broadcasted_iota(jnp.int32, (GMLP_CHUNK, LANES), 1)
    first_head = lane < GMLP_HEAD_DIM
    for c in range(t_len // GMLP_CHUNK):
        r0 = c * GMLP_CHUNK
        tiles = []
        for j in range(cw // LANES):
            vc = v[r0:r0 + GMLP_CHUNK, j * LANES:(j + 1) * LANES]
            tiles.append(jnp.where(first_head, _dot(wm_ref[2 * j], vc), _dot(wm_ref[2 * j + 1], vc)))
        y = jnp.concatenate(tiles, axis=1) + bias_ref[...]
        y_ref[r0:r0 + GMLP_CHUNK, :] = (u[r0:r0 + GMLP_CHUNK, :] * y).astype(y_ref.dtype)


def _gmlp(uv, vnorm_g, ws, bs):
    n = uv.shape[0]
    cw = uv.shape[1] // 2
    heads = ws.shape[0]
    t = min(T_GMLP, n)
    tril = jnp.tril(jnp.ones((GMLP_CHUNK, GMLP_CHUNK), dtype=bool))
    wm = jnp.where(tril, ws, jnp.zeros_like(ws)).astype(BF16)
    bias = jnp.repeat(bs.T.astype(F32), cw // heads, axis=1)
    return pl.pallas_call(
        _gmlp_kernel,
        grid=(n // t,),
        in_specs=[pl.BlockSpec((t, 2 * cw), lambda i: (i, 0)),
                  pl.BlockSpec((1, cw), lambda i: (0, 0)),
                  pl.BlockSpec(wm.shape, lambda i: (0, 0, 0)),
                  pl.BlockSpec(bias.shape, lambda i: (0, 0))],
        out_specs=pl.BlockSpec((t, cw), lambda i: (i, 0)),
        out_shape=jax.ShapeDtypeStruct((n, cw), BF16),
        compiler_params=_cparams(("parallel",)),
        name="gmlp",
    )(uv, vnorm_g.reshape(1, cw).astype(F32), wm, bias)


def _sb_attn_kernel(q_ref, k_ref, v_ref, tri_ref, o_ref, acc_ref, run_ref):
    qi = pl.program_id(2)
    n_sub = q_ref.shape[0] // TK_SB
    q = q_ref[...]
    lane = lax.broadcasted_iota(jnp.int32, q.shape, 1)
    lane_sub = lax.broadcasted_iota(jnp.int32, (TK_SB, LANES), 1)
    q_h0 = jnp.where(lane < SB_HEAD_DIM, q, jnp.zeros_like(q))
    q_h1 = jnp.where(lane < SB_HEAD_DIM, jnp.zeros_like(q), q)
    row = lax.broadcasted_iota(jnp.int32, (2 * TK_SB, TK_SB), 0)
    col = lax.broadcasted_iota(jnp.int32, (2 * TK_SB, TK_SB), 1)
    below_diag = col < jnp.where(row >= TK_SB, row - TK_SB, row)

    def visit(qs, kb, run, acc, diag):
        okf = jnp.where(kb >= 0, 1.0, 0.0).astype(F32)
        k0 = pl.multiple_of(jnp.maximum(kb, 0) * TK_SB, TK_SB)
        kblk = k_ref[pl.ds(k0, TK_SB), :]
        vblk = v_ref[pl.ds(k0, TK_SB), :]
        z = _dot_nt(qs, kblk)
        sp = jnp.maximum(z, 0.0) + jnp.log(1.0 + jnp.exp(-jnp.abs(z)))
        lm = jnp.where(below_diag, -sp, 0.0) if diag else -sp * okf
        lm_hi = lm.astype(BF16)
        lm_lo = (lm - lm_hi.astype(F32)).astype(BF16)
        cs = _dot(jnp.concatenate([lm_hi, lm_lo], axis=0), tri_ref[...])
        cs = cs[:2 * TK_SB] + cs[2 * TK_SB:]
        w = jnp.exp(z - sp + cs[:, :TK_SB] + run)
        w = jnp.where(below_diag, w, 0.0) if diag else w * okf
        return run + cs[:, TK_SB:], acc + _dot(w.astype(BF16), vblk)

    def stacked_q(sub):
        rows = slice(sub * TK_SB, (sub + 1) * TK_SB)
        return jnp.concatenate([q_h0[rows], q_h1[rows]], axis=0)

    live = False
    for sub in range(n_sub):
        qs = stacked_q(sub)
        run = jnp.zeros((2 * TK_SB, TK_SB), F32)
        acc = jnp.zeros((2 * TK_SB, LANES), F32)
        for o in range(SB_STATIC_BLOCKS):
            run, acc = visit(qs, qi * n_sub + sub - o, run, acc, o == 0)
        run_ref[sub] = run
        acc_ref[sub] = acc
        live = jnp.logical_or(live, jnp.max(run) > SB_LOG_CUTOFF)

    def cond(c):
        o, live = c
        return jnp.logical_and(live, o <= qi * n_sub + n_sub - 1)

    def body(c):
        o, _ = c
        live = False
        for sub in range(n_sub):
            run, acc = visit(stacked_q(sub), qi * n_sub + sub - o, run_ref[sub], acc_ref[sub], False)
            run_ref[sub] = run
            acc_ref[sub] = acc
            live = jnp.logical_or(live, jnp.max(run) > SB_LOG_CUTOFF)
        return o + 1, live

    lax.while_loop(cond, body, (SB_STATIC_BLOCKS, live))
    for sub in range(n_sub):
        acc = acc_ref[sub]
        rows = slice(sub * TK_SB, (sub + 1) * TK_SB)
        o_ref[rows, :] = jnp.where(lane_sub < SB_HEAD_DIM, acc[:TK_SB], acc[TK_SB:]).astype(o_ref.dtype)


def _sb_attention(q, k, v, batch):
    n, dw = q.shape
    seq = n // batch
    tq = min(TQ_SB, seq)
    nq = seq // tq
    npair = dw // LANES
    s_idx = jnp.arange(TK_SB)[:, None]
    j_idx = jnp.arange(TK_SB)[None, :]
    tri = jnp.concatenate([(s_idx > j_idx), jnp.ones((TK_SB, TK_SB), bool)], axis=1).astype(BF16)
    return pl.pallas_call(
        _sb_attn_kernel,
        grid=(batch, npair, nq),
        in_specs=[pl.BlockSpec((tq, LANES), lambda b, h, i: (b * nq + i, h)),
                  pl.BlockSpec((seq, LANES), lambda b, h, i: (b, h)),
                  pl.BlockSpec((seq, LANES), lambda b, h, i: (b, h)),
                  pl.BlockSpec(tri.shape, lambda b, h, i: (0, 0))],
        out_specs=pl.BlockSpec((tq, LANES), lambda b, h, i: (b * nq + i, h)),
        out_shape=jax.ShapeDtypeStruct((n, dw), BF16),
        scratch_shapes=[pltpu.VMEM((tq // TK_SB, 2 * TK_SB, LANES), F32),
                        pltpu.VMEM((tq // TK_SB, 2 * TK_SB, TK_SB), F32)],
        compiler_params=_cparams(("parallel", "parallel", "arbitrary")),
        name="sb_attn",
    )(q, k, v, tri)


def _top_values(s, n, rows_out):
    riota = lax.broadcasted_iota(jnp.int32, (rows_out, s.shape[1]), 0)
    vals = jnp.full((rows_out, s.shape[1]), NEG_INF, F32)
    cur = s
    for i in range(n):
        m = jnp.max(cur, axis=0, keepdims=True)
        vals = jnp.where(riota == i, m, vals)
        cur = jnp.where(cur == m, NEG_INF, cur)
    return vals


def _peer_select_kernel(x_ref, ya_ref, yb_ref, wo_ref, g_ref, wq_ref, k1_ref, k2_ref,
                        h_ref, hx_ref, th_ref, e1_ref, s2_ref, e2_ref, s1_ref, s2s_ref):
    t_len = x_ref.shape[0]
    wa = ya_ref.shape[1]

    @pl.when(pl.program_id(1) == 0)
    def _():
        mix = _dot(ya_ref[...], wo_ref[0:wa, :]) + _dot(yb_ref[...], wo_ref[wa:, :])
        h = x_ref[...] + mix
        h_ref[...] = h
        hx_ref[...] = _rms(h, g_ref[...]).astype(BF16)

    q = _dot(hx_ref[...], wq_ref[...])
    half = q.shape[1] // 2
    s1_ref[...] = _dot_nt(k1_ref[0], q[:, :half].astype(BF16))
    s2s_ref[...] = _dot_nt(k2_ref[0], q[:, half:].astype(BF16))

    n = PEER_TOPK + 1
    rows = 24
    i8 = lax.broadcasted_iota(jnp.int32, (8, LANES), 0)

    def chunk(c, carry):
        l0 = pl.multiple_of(c * LANES, LANES)
        s1 = s1_ref[:, pl.ds(l0, LANES)]
        s2 = s2s_ref[:, pl.ds(l0, LANES)]
        v1 = _top_values(s1, n, rows)
        v2 = _top_values(s2, n, rows)
        cands = [v1[0:1] + v2, v1[1:2] + v2[0:8]]
        for a, nb in ((2, 5), (3, 4), (4, 3), (5, 2), (6, 2), (7, 2)):
            cands.append(jnp.where(i8 < nb, v1[a:a + 1] + v2[0:8], NEG_INF))
        cands.append(v1[8:24] + v2[0:1])
        cand = jnp.concatenate(cands, axis=0)
        top = _top_values(cand, n, rows)
        t16 = top[PEER_TOPK - 1:PEER_TOPK]
        t17 = top[PEER_TOPK:PEER_TOPK + 1]
        tau = jnp.where(t17 == NEG_INF, t16, 0.5 * (t16 + t17))
        m1 = v1[0:1]
        m2 = v2[0:1]
        zsum = jnp.sum(jnp.where(cand >= tau, jnp.exp(cand - (m1 + m2)), 0.0), axis=0, keepdims=True)
        th_ref[0, c] = tau - s1
        e1_ref[0, c] = jnp.exp(s1 - m1)
        s2_ref[0, c] = s2
        e2_ref[0, c] = jnp.exp(s2 - m2) * (0.5 / zsum)
        return carry

    lax.fori_loop(0, t_len // LANES, chunk, 0)


def _peer_select(x, ya, yb, w_out, g, wq, k1, k2):
    n, d = x.shape
    wa = ya.shape[1]
    heads, keys, half = k1.shape
    t = min(T_PEER, n)
    sel_shape = jax.ShapeDtypeStruct((heads, n // LANES, keys, LANES), F32)
    sel_spec = pl.BlockSpec((1, t // LANES, keys, LANES), lambda i, h: (h, i, 0, 0))
    return pl.pallas_call(
        _peer_select_kernel,
        grid=(n // t, heads),
        in_specs=[pl.BlockSpec((t, d), lambda i, h: (i, 0)),
                  pl.BlockSpec((t, wa), lambda i, h: (i, 0)),
                  pl.BlockSpec((t, yb.shape[1]), lambda i, h: (i, 0)),
                  pl.BlockSpec(w_out.shape, lambda i, h: (0, 0)),
                  pl.BlockSpec((1, d), lambda i, h: (0, 0)),
                  pl.BlockSpec((d, 2 * half), lambda i, h: (0, h)),
                  pl.BlockSpec((1, keys, half), lambda i, h: (h, 0, 0)),
                  pl.BlockSpec((1, keys, half), lambda i, h: (h, 0, 0))],
        out_specs=[pl.BlockSpec((t, d), lambda i, h: (i, 0)),
                   pl.BlockSpec((t, d), lambda i, h: (i, 0)),
                   sel_spec, sel_spec, sel_spec, sel_spec],
        out_shape=[jax.ShapeDtypeStruct((n, d), F32), jax.ShapeDtypeStruct((n, d), BF16),
                   sel_shape, sel_shape, sel_shape, sel_shape],
        scratch_shapes=[pltpu.VMEM((keys, t), F32), pltpu.VMEM((keys, t), F32)],
        compiler_params=_cparams(("parallel", "arbitrary")),
        name="peer_select",
    )(x, ya, yb, w_out.astype(BF16), g.reshape(1, d).astype(F32), wq.astype(BF16),
      k1.astype(BF16), k2.astype(BF16))


def _peer_dense_kernel(hx_ref, th_ref, e1_ref, s2_ref, e2_ref, u_ref, vt_ref, res_ref, gf_ref,
                       out_ref, acc_ref, act_ref, w_ref, *, final_norm):
    j = pl.program_id(1)
    heads, n_lt, keys, _ = th_ref.shape
    n_i1 = u_ref.shape[0] // keys
    rc = 32
    c0 = math.sqrt(2.0 / math.pi)
    c1 = c0 * 0.044715

    @pl.when(j == 0)
    def _():
        acc_ref[...] = jnp.zeros_like(acc_ref)

    act_ref[...] = _dot_nt(u_ref[...], hx_ref[...])

    def body(it, carry):
        ii = it // n_lt
        lt = it % n_lt
        i1 = j * n_i1 + ii
        r0 = pl.multiple_of(ii * keys, keys)
        lanes = pl.ds(pl.multiple_of(lt * LANES, LANES), LANES)
        th = [jnp.broadcast_to(th_ref[h, lt, pl.ds(i1, 1), :], (rc, LANES)) for h in range(heads)]
        e1 = [jnp.broadcast_to(e1_ref[h, lt, pl.ds(i1, 1), :], (rc, LANES)) for h in range(heads)]
        for r in range(0, keys, rc):
            g = jnp.zeros((rc, LANES), F32)
            for h in range(heads):
                g = g + jnp.where(s2_ref[h, lt, r:r + rc, :] >= th[h], e2_ref[h, lt, r:r + rc, :], 0.0) * e1[h]
            a = act_ref[pl.ds(r0 + r, rc), lanes]
            tanh = jnp.tanh(a * (c0 + c1 * (a * a)))
            ga = g * a
            w_ref[pl.ds(r0 + r, rc), lanes] = (ga + ga * tanh).astype(BF16)
        return carry

    lax.fori_loop(0, n_i1 * n_lt, body, 0)
    acc_ref[...] += _dot(vt_ref[...], w_ref[...])

    @pl.when(j == pl.num_programs(1) - 1)
    def _():
        o = res_ref[...] + acc_ref[...].T
        if final_norm:
            o = _rms(o, gf_ref[...])
        out_ref[...] = o


def _peer_dense(hx, th, e1, s2, e2, u_tab, vt_tab, res, g_final, final_norm):
    n, d = res.shape
    heads, _, keys, _ = th.shape
    n_exp = u_tab.shape[0]
    t = min(T_PEER, n)
    e_blk = min(E_BLK, n_exp)
    nj = n_exp // e_blk
    sel_spec = pl.BlockSpec((heads, t // LANES, keys, LANES), lambda i, j: (0, i, 0, 0))
    return pl.pallas_call(
        functools.partial(_peer_dense_kernel, final_norm=final_norm),
        grid=(n // t, nj),
        in_specs=[pl.BlockSpec((t, d), lambda i, j: (i, 0)),
                  sel_spec, sel_spec, sel_spec, sel_spec,
                  pl.BlockSpec((e_blk, d), lambda i, j: (j, 0)),
                  pl.BlockSpec((d, e_blk), lambda i, j: (0, j)),
                  pl.BlockSpec((t, d), lambda i, j: (i, 0)),
                  pl.BlockSpec((1, d), lambda i, j: (0, 0))],
        out_specs=pl.BlockSpec((t, d), lambda i, j: (i, 0)),
        out_shape=jax.ShapeDtypeStruct((n, d), F32),
        scratch_shapes=[pltpu.VMEM((d, t), F32), pltpu.VMEM((e_blk, t), F32), pltpu.VMEM((e_blk, t), BF16)],
        compiler_params=_cparams(("parallel", "arbitrary")),
        name="peer_dense",
    )(hx, th, e1, s2, e2, u_tab, vt_tab, res, g_final.reshape(1, d).astype(F32))


def _peer_block(x, ya, yb, w_out, g_ffn, wq, k1, k2, u_tab, v_tab, g_final, final_norm):
    h, hx, th, e1, s2, e2 = _peer_select(x, ya, yb, w_out, g_ffn, wq, k1, k2)
    return _peer_dense(hx, th, e1, s2, e2, u_tab.astype(BF16), v_tab.astype(BF16).T, h, g_final,
                       final_norm)


def kernel(x, l0_norm_mix_g, l0_w_in, a_lam_re, a_lam_im, a_log_dt, a_b_re, a_b_im, a_c_re, a_c_im, a_d, a_glu_w, a_glu_b, b_conv_w, l0_w_out, l0_norm_ffn_g, l0_peer_wq, l0_peer_k1, l0_peer_k2, l0_peer_u, l0_peer_v, l1_norm_mix_g, l1_w_in, c_vnorm_g, c_ws, c_bs, l1_w_out, l1_norm_ffn_g, l1_peer_wq, l1_peer_k1, l1_peer_k2, l1_peer_u, l1_peer_v, final_norm_g):
    batch, seq, d = x.shape
    n = batch * seq
    h = x.reshape(n, d)

    (p0,) = _norm_matmul(h, l0_norm_mix_g.astype(F32), l0_w_in.astype(BF16),
                         ((0, l0_w_in.shape[1], 1.0),), (F32,))
    bq, are, aim, cq = _s5_params(a_lam_re, a_lam_im, a_log_dt, a_b_re, a_b_im, a_c_re, a_c_im)
    ya, yb = _even_mixer(p0, batch, bq, are, aim, cq, a_d, a_glu_w, a_glu_b, b_conv_w)
    h = _peer_block(h, ya, yb, l0_w_out, l0_norm_ffn_g, l0_peer_wq, l0_peer_k1, l0_peer_k2,
                    l0_peer_u, l0_peer_v, final_norm_g, False)

    cw = c_vnorm_g.shape[0]
    dw = (l1_w_in.shape[1] - 2 * cw) // 3
    o = 2 * cw
    splits = ((0, o, 1.0), (o, o + dw, SB_HEAD_DIM ** -0.5), (o + dw, o + 2 * dw, 1.0),
              (o + 2 * dw, o + 3 * dw, 1.0))
    uv, q, k, v = _norm_matmul(h, l1_norm_mix_g.astype(F32), l1_w_in.astype(BF16), splits,
                               (F32, BF16, BF16, BF16))
    yc = _gmlp(uv, c_vnorm_g, c_ws, c_bs)
    yd = _sb_attention(q, k, v, batch)
    h = _peer_block(h, yc, yd, l1_w_out, l1_norm_ffn_g, l1_peer_wq, l1_peer_k1, l1_peer_k2,
                    l1_peer_u, l1_peer_v, final_norm_g, True)
    return h.reshape(batch, seq, d)
```

```python
import functools
import math

import jax
import jax.numpy as jnp
from jax import lax
from jax.experimental import pallas as pl
from jax.experimental.pallas import tpu as pltpu

F32 = jnp.float32
BF16 = jnp.bfloat16
NEG_INF = float("-inf")

RMS_EPS = 1e-6
LANES = 128
S5_GROUP = 16
S5_STATE = 64
S5_QUARTERS = 4
PEER_HEADS = 8
PEER_KEYS = 128
PEER_TOPK = 16
SB_HEAD_DIM = 64
GMLP_CHUNK = 128
GMLP_HEAD_DIM = 64
SB_LOG_CUTOFF = -104.0

TM_PROJ = 512
T_S5 = 256
T_GMLP = 512
TQ_SB = 256
TK_SB = 128
SB_STATIC_BLOCKS = 3
T_PEER = 512
E_BLK = 1024
VMEM_LIMIT = 48 * 1024 * 1024


def _cparams(sem, flags=None):
    return pltpu.CompilerParams(dimension_semantics=sem, vmem_limit_bytes=VMEM_LIMIT, flags=flags)


def _dot(a, b):
    return jnp.dot(a, b, preferred_element_type=F32)


def _dot_nt(a, b):
    return lax.dot_general(a, b, (((1,), (1,)), ((), ())), preferred_element_type=F32)


def _gelu(x):
    c = math.sqrt(2.0 / math.pi)
    return 0.5 * x * (1.0 + jnp.tanh(c * (x + 0.044715 * (x * x * x))))


def _rms(x, g):
    return x * lax.rsqrt(jnp.mean(x * x, axis=-1, keepdims=True) + RMS_EPS) * g


def _norm_matmul_kernel(x_ref, g_ref, w_ref, *out_refs, splits):
    xn = _rms(x_ref[...], g_ref[...]).astype(BF16)
    p = _dot(xn, w_ref[...])
    for o_ref, (lo, hi, scale) in zip(out_refs, splits):
        part = p[:, lo:hi]
        if scale != 1.0:
            part = part * scale
        o_ref[...] = part.astype(o_ref.dtype)


def _norm_matmul(x, g, w, splits, dtypes):
    n, d = x.shape
    m = w.shape[1]
    tm = min(TM_PROJ, n)
    out_shape = [jax.ShapeDtypeStruct((n, hi - lo), dt) for (lo, hi, _), dt in zip(splits, dtypes)]
    out_specs = [pl.BlockSpec((tm, hi - lo), lambda i: (i, 0)) for (lo, hi, _) in splits]
    return pl.pallas_call(
        functools.partial(_norm_matmul_kernel, splits=splits),
        grid=(n // tm,),
        in_specs=[pl.BlockSpec((tm, d), lambda i: (i, 0)),
                  pl.BlockSpec((1, d), lambda i: (0, 0)),
                  pl.BlockSpec((d, m), lambda i: (0, 0))],
        out_specs=out_specs,
        out_shape=out_shape,
        compiler_params=_cparams(("parallel",)),
        name="norm_matmul",
    )(x, g.reshape(1, d), w)


def _even_mixer_kernel(p_ref, bq_ref, are_ref, aim_ref, cq_ref, d_ref, gw_ref, gb_ref, cw_ref,
                       ya_ref, yb_ref, bu_ref, hst_ref, zs_ref):
    t_len = p_ref.shape[0]
    aw = ya_ref.shape[1]
    qs = 2 * aw
    half = qs // 2

    @pl.when(pl.program_id(1) == 0)
    def _():
        hst_ref[...] = jnp.zeros_like(hst_ref)
        zs_ref[0:8, :] = jnp.zeros((8, zs_ref.shape[1]), F32)

    u = p_ref[:, 0:aw]
    ub = u.astype(BF16)
    for q in range(S5_QUARTERS):
        bu_ref[:, q * qs:(q + 1) * qs] = _dot(ub[:, q * LANES:(q + 1) * LANES], bq_ref[q])

    for q in range(S5_QUARTERS):
        ar = are_ref[q:q + 1, :]
        ai = aim_ref[q:q + 1, :]
        re_lo = q * qs
        im_lo = q * qs + half

        def step(t, carry, ar=ar, ai=ai, re_lo=re_lo, im_lo=im_lo):
            hr, hi = carry
            br = bu_ref[pl.ds(t, 1), re_lo:re_lo + half]
            bi = bu_ref[pl.ds(t, 1), im_lo:im_lo + half]
            nr = ar * hr - ai * hi + br
            ni = ar * hi + ai * hr + bi
            bu_ref[pl.ds(t, 1), re_lo:re_lo + half] = nr
            bu_ref[pl.ds(t, 1), im_lo:im_lo + half] = ni
            return nr, ni

        hr, hi = lax.fori_loop(0, t_len, step,
                               (hst_ref[2 * q:2 * q + 1, :], hst_ref[2 * q + 1:2 * q + 2, :]), unroll=8)
        hst_ref[2 * q:2 * q + 1, :] = hr
        hst_ref[2 * q + 1:2 * q + 2, :] = hi

    ys = [_dot(bu_ref[:, q * qs:(q + 1) * qs].astype(BF16), cq_ref[q]) for q in range(S5_QUARTERS)]
    y = jnp.concatenate(ys, axis=1) + d_ref[...] * u
    y = _gelu(y)
    gate = jax.nn.sigmoid(_dot(y.astype(BF16), gw_ref[...]) + gb_ref[...])
    ya_ref[...] = (y * gate).astype(ya_ref.dtype)

    bw = yb_ref.shape[1]
    z = p_ref[:, aw + bw:aw + 2 * bw] * p_ref[:, aw + 2 * bw:aw + 3 * bw]
    zs_ref[8:8 + t_len, :] = z
    conv = (cw_ref[2:3, :] * z + cw_ref[1:2, :] * zs_ref[7:7 + t_len, :]
            + cw_ref[0:1, :] * zs_ref[6:6 + t_len, :])
    yb_ref[...] = (p_ref[:, aw:aw + bw] * conv).astype(yb_ref.dtype)
    zs_ref[0:8, :] = zs_ref[t_len:t_len + 8, :]


def _s5_params(lam_re, lam_im, log_dt, b_re, b_im, c_re, c_im):
    g, p = lam_re.shape
    gq = g // S5_QUARTERS
    dt = jnp.exp(log_dt.astype(F32))[:, None]
    lr = lam_re.astype(F32)
    li = lam_im.astype(F32)
    mag = jnp.exp(lr * dt)
    ar = mag * jnp.cos(li * dt)
    ai = mag * jnp.sin(li * dt)
    den = lr * lr + li * li
    nr = ar - 1.0
    fr = (nr * lr + ai * li) / den
    fi = (ai * lr - nr * li) / den
    bre = b_re.astype(F32)
    bim = b_im.astype(F32)
    bbar_re = fr[..., None] * bre - fi[..., None] * bim
    bbar_im = fr[..., None] * bim + fi[..., None] * bre
    eye = jnp.eye(gq, dtype=F32)

    def in_blockdiag(m):
        m = m.reshape(S5_QUARTERS, gq, p, S5_GROUP)
        return jnp.einsum('qgpc,gh->qgchp', m, eye).reshape(S5_QUARTERS, gq * S5_GROUP, gq * p)

    def out_blockdiag(m):
        m = m.reshape(S5_QUARTERS, gq, S5_GROUP, p)
        return jnp.einsum('qgcp,gh->qgphc', m, eye).reshape(S5_QUARTERS, gq * p, gq * S5_GROUP)

    bq = jnp.concatenate([in_blockdiag(bbar_re), in_blockdiag(bbar_im)], axis=2).astype(BF16)
    cq = jnp.concatenate([out_blockdiag(c_re.astype(F32)), -out_blockdiag(c_im.astype(F32))],
                         axis=1).astype(BF16)
    are = ar.reshape(S5_QUARTERS, gq * p)
    aim = ai.reshape(S5_QUARTERS, gq * p)
    return bq, are, aim, cq


def _even_mixer(p, batch, bq, are, aim, cq, d_skip, glu_w, glu_b, conv_w):
    n, width = p.shape
    seq = n // batch
    aw = d_skip.size
    bw = conv_w.shape[1]
    t = min(T_S5, seq)
    nt = seq // t
    qs = 2 * aw
    full = lambda shape: pl.BlockSpec(shape, lambda b, i: (0,) * len(shape))
    return pl.pallas_call(
        _even_mixer_kernel,
        grid=(batch, nt),
        in_specs=[pl.BlockSpec((t, width), lambda b, i: (b * nt + i, 0)),
                  full(bq.shape), full(are.shape), full(aim.shape), full(cq.shape),
                  full((1, aw)), full(glu_w.shape), full((1, aw)), full(conv_w.shape)],
        out_specs=[pl.BlockSpec((t, aw), lambda b, i: (b * nt + i, 0)),
                   pl.BlockSpec((t, bw), lambda b, i: (b * nt + i, 0))],
        out_shape=[jax.ShapeDtypeStruct((n, aw), BF16), jax.ShapeDtypeStruct((n, bw), BF16)],
        scratch_shapes=[pltpu.VMEM((t, S5_QUARTERS * qs), F32),
                        pltpu.VMEM((2 * S5_QUARTERS, aw), F32),
                        pltpu.VMEM((t + 8, bw), F32)],
        compiler_params=_cparams(("arbitrary", "arbitrary")),
        name="even_mixer",
    )(p, bq, are, aim, cq, d_skip.reshape(1, aw).astype(F32), glu_w.astype(BF16),
      glu_b.reshape(1, aw).astype(F32), conv_w.astype(F32))


def _gmlp_kernel(uv_ref, g_ref, wm_ref, bias_ref, y_ref):
    cw = y_ref.shape[1]
    t_len = y_ref.shape[0]
    u = _gelu(uv_ref[:, 0:cw])
    v = _rms(_gelu(uv_ref[:, cw:2 * cw]), g_ref[...]).astype(BF16)
    lane = lax.broadcasted_iota(jnp.int32, (GMLP_CHUNK, LANES), 1)
    first_head = lane < GMLP_HEAD_DIM
    for c in range(t_len // GMLP_CHUNK):
        r0 = c * GMLP_CHUNK
        tiles = []
        for j in range(cw // LANES):
            vc = v[r0:r0 + GMLP_CHUNK, j * LANES:(j + 1) * LANES]
            tiles.append(jnp.where(first_head, _dot(wm_ref[2 * j], vc), _dot(wm_ref[2 * j + 1], vc)))
        y = jnp.concatenate(tiles, axis=1) + bias_ref[...]
        y_ref[r0:r0 + GMLP_CHUNK, :] = (u[r0:r0 + GMLP_CHUNK, :] * y).astype(y_ref.dtype)


def _gmlp(uv, vnorm_g, ws, bs):
    n = uv.shape[0]
    cw = uv.shape[1] // 2
    heads = ws.shape[0]
    t = min(T_GMLP, n)
    tril = jnp.tril(jnp.ones((GMLP_CHUNK, GMLP_CHUNK), dtype=bool))
    wm = jnp.where(tril, ws, jnp.zeros_like(ws)).astype(BF16)
    bias = jnp.repeat(bs.T.astype(F32), cw // heads, axis=1)
    return pl.pallas_call(
        _gmlp_kernel,
        grid=(n // t,),
        in_specs=[pl.BlockSpec((t, 2 * cw), lambda i: (i, 0)),
                  pl.BlockSpec((1, cw), lambda i: (0, 0)),
                  pl.BlockSpec(wm.shape, lambda i: (0, 0, 0)),
                  pl.BlockSpec(bias.shape, lambda i: (0, 0))],
        out_specs=pl.BlockSpec((t, cw), lambda i: (i, 0)),
        out_shape=jax.ShapeDtypeStruct((n, cw), BF16),
        compiler_params=_cparams(("parallel",)),
        name="gmlp",
    )(uv, vnorm_g.reshape(1, cw).astype(F32), wm, bias)


def _sb_attn_kernel(q_ref, k_ref, v_ref, tri_ref, o_ref, acc_ref, run_ref):
    qi = pl.program_id(2)
    n_sub = q_ref.shape[0] // TK_SB
    q = q_ref[...]
    lane = lax.broadcasted_iota(jnp.int32, q.shape, 1)
    lane_sub = lax.broadcasted_iota(jnp.int32, (TK_SB, LANES), 1)
    q_h0 = jnp.where(lane < SB_HEAD_DIM, q, jnp.zeros_like(q))
    q_h1 = jnp.where(lane < SB_HEAD_DIM, jnp.zeros_like(q), q)
    row = lax.broadcasted_iota(jnp.int32, (2 * TK_SB, TK_SB), 0)
    col = lax.broadcasted_iota(jnp.int32, (2 * TK_SB, TK_SB), 1)
    below_diag = col < jnp.where(row >= TK_SB, row - TK_SB, row)

    def visit(qs, kb, run, acc, diag):
        okf = jnp.where(kb >= 0, 1.0, 0.0).astype(F32)
        k0 = pl.multiple_of(jnp.maximum(kb, 0) * TK_SB, TK_SB)
        kblk = k_ref[pl.ds(k0, TK_SB), :]
        vblk = v_ref[pl.ds(k0, TK_SB), :]
        z = _dot_nt(qs, kblk)
        sp = jnp.maximum(z, 0.0) + jnp.log(1.0 + jnp.exp(-jnp.abs(z)))
        lm = jnp.where(below_diag, -sp, 0.0) if diag else -sp * okf
        lm_hi = lm.astype(BF16)
        lm_lo = (lm - lm_hi.astype(F32)).astype(BF16)
        cs = _dot(jnp.concatenate([lm_hi, lm_lo], axis=0), tri_ref[...])
        cs = cs[:2 * TK_SB] + cs[2 * TK_SB:]
        w = jnp.exp(z - sp + cs[:, :TK_SB] + run)
        w = jnp.where(below_diag, w, 0.0) if diag else w * okf
        return run + cs[:, TK_SB:], acc + _dot(w.astype(BF16), vblk)

    def stacked_q(sub):
        rows = slice(sub * TK_SB, (sub + 1) * TK_SB)
        return jnp.concatenate([q_h0[rows], q_h1[rows]], axis=0)

    live = False
    for sub in range(n_sub):
        qs = stacked_q(sub)
        run = jnp.zeros((2 * TK_SB, TK_SB), F32)
        acc = jnp.zeros((2 * TK_SB, LANES), F32)
        for o in range(SB_STATIC_BLOCKS):
            run, acc = visit(qs, qi * n_sub + sub - o, run, acc, o == 0)
        run_ref[sub] = run
        acc_ref[sub] = acc
        live = jnp.logical_or(live, jnp.max(run) > SB_LOG_CUTOFF)

    def cond(c):
        o, live = c
        return jnp.logical_and(live, o <= qi * n_sub + n_sub - 1)

    def body(c):
        o, _ = c
        live = False
        for sub in range(n_sub):
            run, acc = visit(stacked_q(sub), qi * n_sub + sub - o, run_ref[sub], acc_ref[sub], False)
            run_ref[sub] = run
            acc_ref[sub] = acc
            live = jnp.logical_or(live, jnp.max(run) > SB_LOG_CUTOFF)
        return o + 1, live

    lax.while_loop(cond, body, (SB_STATIC_BLOCKS, live))
    for sub in range(n_sub):
        acc = acc_ref[sub]
        rows = slice(sub * TK_SB, (sub + 1) * TK_SB)
        o_ref[rows, :] = jnp.where(lane_sub < SB_HEAD_DIM, acc[:TK_SB], acc[TK_SB:]).astype(o_ref.dtype)


def _sb_attention(q, k, v, batch):
    n, dw = q.shape
    seq = n // batch
    tq = min(TQ_SB, seq)
    nq = seq // tq
    npair = dw // LANES
    s_idx = jnp.arange(TK_SB)[:, None]
    j_idx = jnp.arange(TK_SB)[None, :]
    tri = jnp.concatenate([(s_idx > j_idx), jnp.ones((TK_SB, TK_SB), bool)], axis=1).astype(BF16)
    return pl.pallas_call(
        _sb_attn_kernel,
        grid=(batch, npair, nq),
        in_specs=[pl.BlockSpec((tq, LANES), lambda b, h, i: (b * nq + i, h)),
                  pl.BlockSpec((seq, LANES), lambda b, h, i: (b, h)),
                  pl.BlockSpec((seq, LANES), lambda b, h, i: (b, h)),
                  pl.BlockSpec(tri.shape, lambda b, h, i: (0, 0))],
        out_specs=pl.BlockSpec((tq, LANES), lambda b, h, i: (b * nq + i, h)),
        out_shape=jax.ShapeDtypeStruct((n, dw), BF16),
        scratch_shapes=[pltpu.VMEM((tq // TK_SB, 2 * TK_SB, LANES), F32),
                        pltpu.VMEM((tq // TK_SB, 2 * TK_SB, TK_SB), F32)],
        compiler_params=_cparams(("parallel", "parallel", "arbitrary")),
        name="sb_attn",
    )(q, k, v, tri)


def _top_values(s, n, with_rank=False):
    riota = lax.broadcasted_iota(jnp.int32, (n, s.shape[1]), 0)
    vals = jnp.full((n, s.shape[1]), NEG_INF, F32)
    rank = jnp.full(s.shape, float(n), F32)
    cur = s
    for i in range(n):
        m = jnp.max(cur, axis=0, keepdims=True)
        vals = jnp.where(riota == i, m, vals)
        hit = cur == m
        if with_rank:
            rank = jnp.where(hit, float(i), rank)
        cur = jnp.where(hit, NEG_INF, cur)
    return (vals, rank) if with_rank else vals


def _peer_select_kernel(x_ref, ya_ref, yb_ref, wo_ref, g_ref, wq_ref, k1_ref, k2_ref,
                        h_ref, hx_ref, cnt_ref, e1_ref, rk_ref, e2_ref, s1_ref, s2s_ref):
    t_len = x_ref.shape[0]
    wa = ya_ref.shape[1]

    @pl.when(pl.program_id(1) == 0)
    def _():
        mix = _dot(ya_ref[...], wo_ref[0:wa, :]) + _dot(yb_ref[...], wo_ref[wa:, :])
        h = x_ref[...] + mix
        h_ref[...] = h
        hx_ref[...] = _rms(h, g_ref[...]).astype(BF16)

    q = _dot(hx_ref[...], wq_ref[...])
    half = q.shape[1] // 2
    s1_ref[...] = _dot_nt(k1_ref[0], q[:, :half].astype(BF16))
    s2s_ref[...] = _dot_nt(k2_ref[0], q[:, half:].astype(BF16))

    n = PEER_TOPK
    i8 = lax.broadcasted_iota(jnp.int32, (8, LANES), 0)

    def chunk(c, carry):
        l0 = pl.multiple_of(c * LANES, LANES)
        s1 = s1_ref[:, pl.ds(l0, LANES)]
        s2 = s2s_ref[:, pl.ds(l0, LANES)]
        v1 = _top_values(s1, n)
        v2, rank2 = _top_values(s2, n, with_rank=True)
        cands = [v1[0:1] + v2, v1[1:2] + v2[0:8]]
        for a, nb in ((2, 5), (3, 4), (4, 3), (5, 2), (6, 2), (7, 2)):
            cands.append(jnp.where(i8 < nb, v1[a:a + 1] + v2[0:8], NEG_INF))
        cands.append(v1[8:16] + v2[0:1])
        cand = jnp.concatenate(cands, axis=0)
        top = _top_values(cand, n)
        tau = jnp.min(jnp.where(top > NEG_INF, top, float("inf")), axis=0, keepdims=True)
        m1 = v1[0:1]
        m2 = v2[0:1]
        zsum = jnp.sum(jnp.where(cand >= tau, jnp.exp(cand - (m1 + m2)), 0.0), axis=0, keepdims=True)
        cnt = jnp.zeros_like(s1)
        for b in range(n):
            cnt = cnt + jnp.where(s1 + v2[b:b + 1] >= tau, 1.0, 0.0)
        cnt_ref[0, c] = cnt
        e1_ref[0, c] = jnp.exp(s1 - m1)
        rk_ref[0, c] = rank2.astype(BF16)
        e2_ref[0, c] = (jnp.exp(s2 - m2) * (0.5 / zsum)).astype(BF16)
        return carry

    lax.fori_loop(0, t_len // LANES, chunk, 0)


def _peer_select(x, ya, yb, w_out, g, wq, k1, k2):
    n, d = x.shape
    wa = ya.shape[1]
    heads, keys, half = k1.shape
    t = min(T_PEER, n)
    sel_f32 = jax.ShapeDtypeStruct((heads, n // LANES, keys, LANES), F32)
    sel_bf16 = jax.ShapeDtypeStruct((heads, n // LANES, keys, LANES), BF16)
    sel_spec = pl.BlockSpec((1, t // LANES, keys, LANES), lambda i, h: (h, i, 0, 0))
    return pl.pallas_call(
        _peer_select_kernel,
        grid=(n // t, heads),
        in_specs=[pl.BlockSpec((t, d), lambda i, h: (i, 0)),
                  pl.BlockSpec((t, wa), lambda i, h: (i, 0)),
                  pl.BlockSpec((t, yb.shape[1]), lambda i, h: (i, 0)),
                  pl.BlockSpec(w_out.shape, lambda i, h: (0, 0)),
                  pl.BlockSpec((1, d), lambda i, h: (0, 0)),
                  pl.BlockSpec((d, 2 * half), lambda i, h: (0, h)),
                  pl.BlockSpec((1, keys, half), lambda i, h: (h, 0, 0)),
                  pl.BlockSpec((1, keys, half), lambda i, h: (h, 0, 0))],
        out_specs=[pl.BlockSpec((t, d), lambda i, h: (i, 0)),
                   pl.BlockSpec((t, d), lambda i, h: (i, 0)),
                   sel_spec, sel_spec, sel_spec, sel_spec],
        out_shape=[jax.ShapeDtypeStruct((n, d), F32), jax.ShapeDtypeStruct((n, d), BF16),
                   sel_f32, sel_f32, sel_bf16, sel_bf16],
        scratch_shapes=[pltpu.VMEM((keys, t), F32), pltpu.VMEM((keys, t), F32)],
        compiler_params=_cparams(("parallel", "arbitrary")),
        name="peer_select",
    )(x, ya, yb, w_out.astype(BF16), g.reshape(1, d).astype(F32), wq.astype(BF16),
      k1.astype(BF16), k2.astype(BF16))


def _peer_dense_kernel(hx_ref, cnt_ref, e1_ref, rk_ref, e2_ref, u_ref, vt_ref, res_ref, gf_ref,
                       out_ref, acc_ref, act_ref, w_ref, rks_ref, e2s_ref, *, final_norm):
    j = pl.program_id(1)
    heads, n_lt, keys, _ = cnt_ref.shape
    n_i1 = u_ref.shape[0] // keys
    rc = keys
    c0 = math.sqrt(2.0 / math.pi)
    c1 = c0 * 0.044715

    @pl.when(j == 0)
    def _():
        acc_ref[...] = jnp.zeros_like(acc_ref)
        rks_ref[...] = rk_ref[...]
        e2s_ref[...] = e2_ref[...]

    act_ref[...] = _dot_nt(u_ref[...], hx_ref[...]).astype(BF16)

    def body(it, carry):
        ii = it // n_lt
        lt = it % n_lt
        i1 = j * n_i1 + ii
        r0 = pl.multiple_of(ii * keys, keys)
        lanes = pl.ds(pl.multiple_of(lt * LANES, LANES), LANES)
        cnt = [jnp.broadcast_to(cnt_ref[h, lt, pl.ds(i1, 1), :], (rc, LANES)).astype(BF16) for h in range(heads)]
        e1 = [jnp.broadcast_to(e1_ref[h, lt, pl.ds(i1, 1), :], (rc, LANES)).astype(BF16) for h in range(heads)]
        zero = jnp.zeros((rc, LANES), BF16)
        for r in range(0, keys, rc):
            g = zero
            for h in range(heads):
                g = g + jnp.where(rks_ref[h, lt, r:r + rc, :] < cnt[h], e2s_ref[h, lt, r:r + rc, :], zero) * e1[h]
            a = act_ref[pl.ds(r0 + r, rc), lanes]
            tanh = jnp.tanh(a * (c0 + c1 * (a * a)))
            ga = g * a
            w_ref[pl.ds(r0 + r, rc), lanes] = ga + ga * tanh
        return carry

    lax.fori_loop(0, n_i1 * n_lt, body, 0, unroll=2)
    acc_ref[...] += _dot(vt_ref[...], w_ref[...])

    @pl.when(j == pl.num_programs(1) - 1)
    def _():
        o = res_ref[...] + acc_ref[...].T
        if final_norm:
            o = _rms(o, gf_ref[...])
        out_ref[...] = o


def _peer_dense(hx, cnt, e1, rk, e2, u_tab, vt_tab, res, g_final, final_norm):
    n, d = res.shape
    heads, _, keys, _ = cnt.shape
    n_exp = u_tab.shape[0]
    t = min(T_PEER, n)
    e_blk = min(E_BLK, n_exp)
    nj = n_exp // e_blk
    sel_spec = pl.BlockSpec((heads, t // LANES, keys, LANES), lambda i, j: (0, i, 0, 0))
    return pl.pallas_call(
        functools.partial(_peer_dense_kernel, final_norm=final_norm),
        grid=(n // t, nj),
        in_specs=[pl.BlockSpec((t, d), lambda i, j: (i, 0)),
                  sel_spec, sel_spec, sel_spec, sel_spec,
                  pl.BlockSpec((e_blk, d), lambda i, j: (j, 0)),
                  pl.BlockSpec((d, e_blk), lambda i, j: (0, j)),
                  pl.BlockSpec((t, d), lambda i, j: (i, 0)),
                  pl.BlockSpec((1, d), lambda i, j: (0, 0))],
        out_specs=pl.BlockSpec((t, d), lambda i, j: (i, 0)),
        out_shape=jax.ShapeDtypeStruct((n, d), F32),
        scratch_shapes=[pltpu.VMEM((d, t), F32), pltpu.VMEM((e_blk, t), BF16), pltpu.VMEM((e_blk, t), BF16),
                        pltpu.VMEM((heads, t // LANES, keys, LANES), BF16),
                        pltpu.VMEM((heads, t // LANES, keys, LANES), BF16)],
        compiler_params=_cparams(("parallel", "arbitrary")),
        name="peer_dense",
    )(hx, cnt, e1, rk, e2, u_tab, vt_tab, res, g_final.reshape(1, d).astype(F32))


def _peer_block(x, ya, yb, w_out, g_ffn, wq, k1, k2, u_tab, v_tab, g_final, final_norm):
    h, hx, cnt, e1, rk, e2 = _peer_select(x, ya, yb, w_out, g_ffn, wq, k1, k2)
    return _peer_dense(hx, cnt, e1, rk, e2, u_tab.astype(BF16), v_tab.astype(BF16).T, h, g_final,
                       final_norm)


def kernel(x, l0_norm_mix_g, l0_w_in, a_lam_re, a_lam_im, a_log_dt, a_b_re, a_b_im, a_c_re, a_c_im, a_d, a_glu_w, a_glu_b, b_conv_w, l0_w_out, l0_norm_ffn_g, l0_peer_wq, l0_peer_k1, l0_peer_k2, l0_peer_u, l0_peer_v, l1_norm_mix_g, l1_w_in, c_vnorm_g, c_ws, c_bs, l1_w_out, l1_norm_ffn_g, l1_peer_wq, l1_peer_k1, l1_peer_k2, l1_peer_u, l1_peer_v, final_norm_g):
    batch, seq, d = x.shape
    n = batch * seq
    h = x.reshape(n, d)

    (p0,) = _norm_matmul(h, l0_norm_mix_g.astype(F32), l0_w_in.astype(BF16),
                         ((0, l0_w_in.shape[1], 1.0),), (F32,))
    bq, are, aim, cq = _s5_params(a_lam_re, a_lam_im, a_log_dt, a_b_re, a_b_im, a_c_re, a_c_im)
    ya, yb = _even_mixer(p0, batch, bq, are, aim, cq, a_d, a_glu_w, a_glu_b, b_conv_w)
    h = _peer_block(h, ya, yb, l0_w_out, l0_norm_ffn_g, l0_peer_wq, l0_peer_k1, l0_peer_k2,
                    l0_peer_u, l0_peer_v, final_norm_g, False)

    cw = c_vnorm_g.shape[0]
    dw = (l1_w_in.shape[1] - 2 * cw) // 3
    o = 2 * cw
    splits = ((0, o, 1.0), (o, o + dw, SB_HEAD_DIM ** -0.5), (o + dw, o + 2 * dw, 1.0),
              (o + 2 * dw, o + 3 * dw, 1.0))
    uv, q, k, v = _norm_matmul(h, l1_norm_mix_g.astype(F32), l1_w_in.astype(BF16), splits,
                               (F32, BF16, BF16, BF16))
    yc = _gmlp(uv, c_vnorm_g, c_ws, c_bs)
    yd = _sb_attention(q, k, v, batch)
    h = _peer_block(h, yc, yd, l1_w_out, l1_norm_ffn_g, l1_peer_wq, l1_peer_k1, l1_peer_k2,
                    l1_peer_u, l1_peer_v, final_norm_g, True)
    return h.reshape(batch, seq, d)
```

```python
import functools
import math

import jax
import jax.numpy as jnp
from jax import lax
from jax.experimental import pallas as pl
from jax.experimental.pallas import tpu as pltpu

F32 = jnp.float32
BF16 = jnp.bfloat16
NEG_INF = float("-inf")

RMS_EPS = 1e-6
LANES = 128
S5_GROUP = 16
S5_STATE = 64
S5_QUARTERS = 4
PEER_HEADS = 8
PEER_KEYS = 128
PEER_TOPK = 16
SB_HEAD_DIM = 64
GMLP_CHUNK = 128
GMLP_HEAD_DIM = 64
SB_LOG_CUTOFF = -104.0

TM_PROJ = 512
T_S5 = 256
T_GMLP = 512
TQ_SB = 256
TK_SB = 128
SB_STATIC_BLOCKS = 3
T_PEER = 512
E_BLK = 1024
VMEM_LIMIT = 48 * 1024 * 1024


def _cparams(sem, flags=None):
    return pltpu.CompilerParams(dimension_semantics=sem, vmem_limit_bytes=VMEM_LIMIT, flags=flags)


def _dot(a, b):
    return jnp.dot(a, b, preferred_element_type=F32)


def _dot_nt(a, b):
    return lax.dot_general(a, b, (((1,), (1,)), ((), ())), preferred_element_type=F32)


def _gelu(x):
    c = math.sqrt(2.0 / math.pi)
    return 0.5 * x * (1.0 + jnp.tanh(c * (x + 0.044715 * (x * x * x))))


def _rms(x, g):
    return x * lax.rsqrt(jnp.mean(x * x, axis=-1, keepdims=True) + RMS_EPS) * g


def _norm_matmul_kernel(x_ref, g_ref, w_ref, *out_refs, splits):
    xn = _rms(x_ref[...], g_ref[...]).astype(BF16)
    p = _dot(xn, w_ref[...])
    for o_ref, (lo, hi, scale) in zip(out_refs, splits):
        part = p[:, lo:hi]
        if scale != 1.0:
            part = part * scale
        o_ref[...] = part.astype(o_ref.dtype)


def _norm_matmul(x, g, w, splits, dtypes):
    n, d = x.shape
    m = w.shape[1]
    tm = min(TM_PROJ, n)
    out_shape = [jax.ShapeDtypeStruct((n, hi - lo), dt) for (lo, hi, _), dt in zip(splits, dtypes)]
    out_specs = [pl.BlockSpec((tm, hi - lo), lambda i: (i, 0)) for (lo, hi, _) in splits]
    return pl.pallas_call(
        functools.partial(_norm_matmul_kernel, splits=splits),
        grid=(n // tm,),
        in_specs=[pl.BlockSpec((tm, d), lambda i: (i, 0)),
                  pl.BlockSpec((1, d), lambda i: (0, 0)),
                  pl.BlockSpec((d, m), lambda i: (0, 0))],
        out_specs=out_specs,
        out_shape=out_shape,
        compiler_params=_cparams(("parallel",)),
        name="norm_matmul",
    )(x, g.reshape(1, d), w)


def _even_mixer_kernel(p_ref, bq_ref, are_ref, aim_ref, cq_ref, d_ref, gw_ref, gb_ref, cw_ref,
                       ya_ref, yb_ref, bu_ref, hst_ref, zs_ref):
    t_len = p_ref.shape[0]
    aw = ya_ref.shape[1]
    qs = 2 * aw
    half = qs // 2

    @pl.when(pl.program_id(1) == 0)
    def _():
        hst_ref[...] = jnp.zeros_like(hst_ref)
        zs_ref[0:8, :] = jnp.zeros((8, zs_ref.shape[1]), F32)

    u = p_ref[:, 0:aw]
    ub = u.astype(BF16)
    for q in range(S5_QUARTERS):
        bu_ref[:, q * qs:(q + 1) * qs] = _dot(ub[:, q * LANES:(q + 1) * LANES], bq_ref[q])

    for q in range(S5_QUARTERS):
        ar = are_ref[q:q + 1, :]
        ai = aim_ref[q:q + 1, :]
        re_lo = q * qs
        im_lo = q * qs + half

        def step(t, carry, ar=ar, ai=ai, re_lo=re_lo, im_lo=im_lo):
            hr, hi = carry
            br = bu_ref[pl.ds(t, 1), re_lo:re_lo + half]
            bi = bu_ref[pl.ds(t, 1), im_lo:im_lo + half]
            nr = ar * hr - ai * hi + br
            ni = ar * hi + ai * hr + bi
            bu_ref[pl.ds(t, 1), re_lo:re_lo + half] = nr
            bu_ref[pl.ds(t, 1), im_lo:im_lo + half] = ni
            return nr, ni

        hr, hi = lax.fori_loop(0, t_len, step,
                               (hst_ref[2 * q:2 * q + 1, :], hst_ref[2 * q + 1:2 * q + 2, :]), unroll=8)
        hst_ref[2 * q:2 * q + 1, :] = hr
        hst_ref[2 * q + 1:2 * q + 2, :] = hi

    ys = [_dot(bu_ref[:, q * qs:(q + 1) * qs].astype(BF16), cq_ref[q]) for q in range(S5_QUARTERS)]
    y = jnp.concatenate(ys, axis=1) + d_ref[...] * u
    y = _gelu(y)
    gate = jax.nn.sigmoid(_dot(y.astype(BF16), gw_ref[...]) + gb_ref[...])
    ya_ref[...] = (y * gate).astype(ya_ref.dtype)

    bw = yb_ref.shape[1]
    z = p_ref[:, aw + bw:aw + 2 * bw] * p_ref[:, aw + 2 * bw:aw + 3 * bw]
    zs_ref[8:8 + t_len, :] = z
    conv = (cw_ref[2:3, :] * z + cw_ref[1:2, :] * zs_ref[7:7 + t_len, :]
            + cw_ref[0:1, :] * zs_ref[6:6 + t_len, :])
    yb_ref[...] = (p_ref[:, aw:aw + bw] * conv).astype(yb_ref.dtype)
    zs_ref[0:8, :] = zs_ref[t_len:t_len + 8, :]


def _s5_params(lam_re, lam_im, log_dt, b_re, b_im, c_re, c_im):
    g, p = lam_re.shape
    gq = g // S5_QUARTERS
    dt = jnp.exp(log_dt.astype(F32))[:, None]
    lr = lam_re.astype(F32)
    li = lam_im.astype(F32)
    mag = jnp.exp(lr * dt)
    ar = mag * jnp.cos(li * dt)
    ai = mag * jnp.sin(li * dt)
    den = lr * lr + li * li
    nr = ar - 1.0
    fr = (nr * lr + ai * li) / den
    fi = (ai * lr - nr * li) / den
    bre = b_re.astype(F32)
    bim = b_im.astype(F32)
    bbar_re = fr[..., None] * bre - fi[..., None] * bim
    bbar_im = fr[..., None] * bim + fi[..., None] * bre
    eye = jnp.eye(gq, dtype=F32)

    def in_blockdiag(m):
        m = m.reshape(S5_QUARTERS, gq, p, S5_GROUP)
        return jnp.einsum('qgpc,gh->qgchp', m, eye).reshape(S5_QUARTERS, gq * S5_GROUP, gq * p)

    def out_blockdiag(m):
        m = m.reshape(S5_QUARTERS, gq, S5_GROUP, p)
        return jnp.einsum('qgcp,gh->qgphc', m, eye).reshape(S5_QUARTERS, gq * p, gq * S5_GROUP)

    bq = jnp.concatenate([in_blockdiag(bbar_re), in_blockdiag(bbar_im)], axis=2).astype(BF16)
    cq = jnp.concatenate([out_blockdiag(c_re.astype(F32)), -out_blockdiag(c_im.astype(F32))],
                         axis=1).astype(BF16)
    are = ar.reshape(S5_QUARTERS, gq * p)
    aim = ai.reshape(S5_QUARTERS, gq * p)
    return bq, are, aim, cq


def _even_mixer(p, batch, bq, are, aim, cq, d_skip, glu_w, glu_b, conv_w):
    n, width = p.shape
    seq = n // batch
    aw = d_skip.size
    bw = conv_w.shape[1]
    t = min(T_S5, seq)
    nt = seq // t
    qs = 2 * aw
    full = lambda shape: pl.BlockSpec(shape, lambda b, i: (0,) * len(shape))
    return pl.pallas_call(
        _even_mixer_kernel,
        grid=(batch, nt),
        in_specs=[pl.BlockSpec((t, width), lambda b, i: (b * nt + i, 0)),
                  full(bq.shape), full(are.shape), full(aim.shape), full(cq.shape),
                  full((1, aw)), full(glu_w.shape), full((1, aw)), full(conv_w.shape)],
        out_specs=[pl.BlockSpec((t, aw), lambda b, i: (b * nt + i, 0)),
                   pl.BlockSpec((t, bw), lambda b, i: (b * nt + i, 0))],
        out_shape=[jax.ShapeDtypeStruct((n, aw), BF16), jax.ShapeDtypeStruct((n, bw), BF16)],
        scratch_shapes=[pltpu.VMEM((t, S5_QUARTERS * qs), F32),
                        pltpu.VMEM((2 * S5_QUARTERS, aw), F32),
                        pltpu.VMEM((t + 8, bw), F32)],
        compiler_params=_cparams(("arbitrary", "arbitrary")),
        name="even_mixer",
    )(p, bq, are, aim, cq, d_skip.reshape(1, aw).astype(F32), glu_w.astype(BF16),
      glu_b.reshape(1, aw).astype(F32), conv_w.astype(F32))


def _gmlp_kernel(uv_ref, g_ref, wm_ref, bias_ref, y_ref):
    cw = y_ref.shape[1]
    t_len = y_ref.shape[0]
    u = _gelu(uv_ref[:, 0:cw])
    v = _rms(_gelu(uv_ref[:, cw:2 * cw]), g_ref[...]).astype(BF16)
    lane = lax.broadcasted_iota(jnp.int32, (GMLP_CHUNK, LANES), 1)
    first_head = lane < GMLP_HEAD_DIM
    for c in range(t_len // GMLP_CHUNK):
        r0 = c * GMLP_CHUNK
        tiles = []
        for j in range(cw // LANES):
            vc = v[r0:r0 + GMLP_CHUNK, j * LANES:(j + 1) * LANES]
            tiles.append(jnp.where(first_head, _dot(wm_ref[2 * j], vc), _dot(wm_ref[2 * j + 1], vc)))
        y = jnp.concatenate(tiles, axis=1) + bias_ref[...]
        y_ref[r0:r0 + GMLP_CHUNK, :] = (u[r0:r0 + GMLP_CHUNK, :] * y).astype(y_ref.dtype)


def _gmlp(uv, vnorm_g, ws, bs):
    n = uv.shape[0]
    cw = uv.shape[1] // 2
    heads = ws.shape[0]
    t = min(T_GMLP, n)
    tril = jnp.tril(jnp.ones((GMLP_CHUNK, GMLP_CHUNK), dtype=bool))
    wm = jnp.where(tril, ws, jnp.zeros_like(ws)).astype(BF16)
    bias = jnp.repeat(bs.T.astype(F32), cw // heads, axis=1)
    return pl.pallas_call(
        _gmlp_kernel,
        grid=(n // t,),
        in_specs=[pl.BlockSpec((t, 2 * cw), lambda i: (i, 0)),
                  pl.BlockSpec((1, cw), lambda i: (0, 0)),
                  pl.BlockSpec(wm.shape, lambda i: (0, 0, 0)),
                  pl.BlockSpec(bias.shape, lambda i: (0, 0))],
        out_specs=pl.BlockSpec((t, cw), lambda i: (i, 0)),
        out_shape=jax.ShapeDtypeStruct((n, cw), BF16),
        compiler_params=_cparams(("parallel",)),
        name="gmlp",
    )(uv, vnorm_g.reshape(1, cw).astype(F32), wm, bias)


def _sb_attn_kernel(q_ref, k_ref, v_ref, tri_ref, o_ref, acc_ref, run_ref):
    qi = pl.program_id(2)
    n_sub = q_ref.shape[0] // TK_SB
    q = q_ref[...]
    lane = lax.broadcasted_iota(jnp.int32, q.shape, 1)
    lane_sub = lax.broadcasted_iota(jnp.int32, (TK_SB, LANES), 1)
    q_h0 = jnp.where(lane < SB_HEAD_DIM, q, jnp.zeros_like(q))
    q_h1 = jnp.where(lane < SB_HEAD_DIM, jnp.zeros_like(q), q)
    row = lax.broadcasted_iota(jnp.int32, (2 * TK_SB, TK_SB), 0)
    col = lax.broadcasted_iota(jnp.int32, (2 * TK_SB, TK_SB), 1)
    below_diag = col < jnp.where(row >= TK_SB, row - TK_SB, row)

    def visit(qs, kb, run, acc, diag):
        okf = jnp.where(kb >= 0, 1.0, 0.0).astype(F32)
        k0 = pl.multiple_of(jnp.maximum(kb, 0) * TK_SB, TK_SB)
        kblk = k_ref[pl.ds(k0, TK_SB), :]
        vblk = v_ref[pl.ds(k0, TK_SB), :]
        z = _dot_nt(qs, kblk)
        sp = jnp.maximum(z, 0.0) + jnp.log(1.0 + jnp.exp(-jnp.abs(z)))
        lm = jnp.where(below_diag, -sp, 0.0) if diag else -sp * okf
        lm_hi = lm.astype(BF16)
        lm_lo = (lm - lm_hi.astype(F32)).astype(BF16)
        cs = _dot(jnp.concatenate([lm_hi, lm_lo], axis=0), tri_ref[...])
        cs = cs[:2 * TK_SB] + cs[2 * TK_SB:]
        w = jnp.exp(z - sp + cs[:, :TK_SB] + run)
        w = jnp.where(below_diag, w, 0.0) if diag else w * okf
        return run + cs[:, TK_SB:], acc + _dot(w.astype(BF16), vblk)

    def stacked_q(sub):
        rows = slice(sub * TK_SB, (sub + 1) * TK_SB)
        return jnp.concatenate([q_h0[rows], q_h1[rows]], axis=0)

    live = False
    for sub in range(n_sub):
        qs = stacked_q(sub)
        run = jnp.zeros((2 * TK_SB, TK_SB), F32)
        acc = jnp.zeros((2 * TK_SB, LANES), F32)
        for o in range(SB_STATIC_BLOCKS):
            run, acc = visit(qs, qi * n_sub + sub - o, run, acc, o == 0)
        run_ref[sub] = run
        acc_ref[sub] = acc
        live = jnp.logical_or(live, jnp.max(run) > SB_LOG_CUTOFF)

    def cond(c):
        o, live = c
        return jnp.logical_and(live, o <= qi * n_sub + n_sub - 1)

    def body(c):
        o, _ = c
        live = False
        for sub in range(n_sub):
            run, acc = visit(stacked_q(sub), qi * n_sub + sub - o, run_ref[sub], acc_ref[sub], False)
            run_ref[sub] = run
            acc_ref[sub] = acc
            live = jnp.logical_or(live, jnp.max(run) > SB_LOG_CUTOFF)
        return o + 1, live

    lax.while_loop(cond, body, (SB_STATIC_BLOCKS, live))
    for sub in range(n_sub):
        acc = acc_ref[sub]
        rows = slice(sub * TK_SB, (sub + 1) * TK_SB)
        o_ref[rows, :] = jnp.where(lane_sub < SB_HEAD_DIM, acc[:TK_SB], acc[TK_SB:]).astype(o_ref.dtype)


def _sb_attention(q, k, v, batch):
    n, dw = q.shape
    seq = n // batch
    tq = min(TQ_SB, seq)
    nq = seq // tq
    npair = dw // LANES
    s_idx = jnp.arange(TK_SB)[:, None]
    j_idx = jnp.arange(TK_SB)[None, :]
    tri = jnp.concatenate([(s_idx > j_idx), jnp.ones((TK_SB, TK_SB), bool)], axis=1).astype(BF16)
    return pl.pallas_call(
        _sb_attn_kernel,
        grid=(batch, npair, nq),
        in_specs=[pl.BlockSpec((tq, LANES), lambda b, h, i: (b * nq + i, h)),
                  pl.BlockSpec((seq, LANES), lambda b, h, i: (b, h)),
                  pl.BlockSpec((seq, LANES), lambda b, h, i: (b, h)),
                  pl.BlockSpec(tri.shape, lambda b, h, i: (0, 0))],
        out_specs=pl.BlockSpec((tq, LANES), lambda b, h, i: (b * nq + i, h)),
        out_shape=jax.ShapeDtypeStruct((n, dw), BF16),
        scratch_shapes=[pltpu.VMEM((tq // TK_SB, 2 * TK_SB, LANES), F32),
                        pltpu.VMEM((tq // TK_SB, 2 * TK_SB, TK_SB), F32)],
        compiler_params=_cparams(("parallel", "parallel", "arbitrary")),
        name="sb_attn",
    )(q, k, v, tri)


def _top_values(s, n, rows_out):
    riota = lax.broadcasted_iota(jnp.int32, (rows_out, s.shape[1]), 0)
    vals = jnp.full((rows_out, s.shape[1]), NEG_INF, F32)
    cur = s
    for i in range(n):
        m = jnp.max(cur, axis=0, keepdims=True)
        vals = jnp.where(riota == i, m, vals)
        cur = jnp.where(cur == m, NEG_INF, cur)
    return vals


def _peer_select_kernel(x_ref, ya_ref, yb_ref, wo_ref, g_ref, wq_ref, k1_ref, k2_ref,
                        h_ref, hx_ref, th_ref, e1_ref, s2_ref, e2_ref, s1_ref, s2s_ref):
    t_len = x_ref.shape[0]
    wa = ya_ref.shape[1]

    @pl.when(pl.program_id(1) == 0)
    def _():
        mix = _dot(ya_ref[...], wo_ref[0:wa, :]) + _dot(yb_ref[...], wo_ref[wa:, :])
        h = x_ref[...] + mix
        h_ref[...] = h
        hx_ref[...] = _rms(h, g_ref[...]).astype(BF16)

    q = _dot(hx_ref[...], wq_ref[...])
    half = q.shape[1] // 2
    s1_ref[...] = _dot_nt(k1_ref[0], q[:, :half].astype(BF16))
    s2s_ref[...] = _dot_nt(k2_ref[0], q[:, half:].astype(BF16))

    n = PEER_TOPK + 1
    rows = 24
    i8 = lax.broadcasted_iota(jnp.int32, (8, LANES), 0)

    def chunk(c, carry):
        l0 = pl.multiple_of(c * LANES, LANES)
        s1 = s1_ref[:, pl.ds(l0, LANES)]
        s2 = s2s_ref[:, pl.ds(l0, LANES)]
        v1 = _top_values(s1, n, rows)
        v2 = _top_values(s2, n, rows)
        cands = [v1[0:1] + v2, v1[1:2] + v2[0:8]]
        for a, nb in ((2, 5), (3, 4), (4, 3), (5, 2), (6, 2), (7, 2)):
            cands.append(jnp.where(i8 < nb, v1[a:a + 1] + v2[0:8], NEG_INF))
        cands.append(v1[8:24] + v2[0:1])
        cand = jnp.concatenate(cands, axis=0)
        top = _top_values(cand, n, rows)
        t16 = top[PEER_TOPK - 1:PEER_TOPK]
        t17 = top[PEER_TOPK:PEER_TOPK + 1]
        tau = jnp.where(t17 == NEG_INF, t16, 0.5 * (t16 + t17))
        m1 = v1[0:1]
        m2 = v2[0:1]
        zsum = jnp.sum(jnp.where(cand >= tau, jnp.exp(cand - (m1 + m2)), 0.0), axis=0, keepdims=True)
        th_ref[0, c] = tau - s1
        e1_ref[0, c] = jnp.exp(s1 - m1)
        s2_ref[0, c] = s2
        e2_ref[0, c] = jnp.exp(s2 - m2) * (0.5 / zsum)
        return carry

    lax.fori_loop(0, t_len // LANES, chunk, 0, unroll=2)


def _peer_select(x, ya, yb, w_out, g, wq, k1, k2):
    n, d = x.shape
    wa = ya.shape[1]
    heads, keys, half = k1.shape
    t = min(T_PEER, n)
    sel_shape = jax.ShapeDtypeStruct((heads, n // LANES, keys, LANES), F32)
    sel_spec = pl.BlockSpec((1, t // LANES, keys, LANES), lambda i, h: (h, i, 0, 0))
    return pl.pallas_call(
        _peer_select_kernel,
        grid=(n // t, heads),
        in_specs=[pl.BlockSpec((t, d), lambda i, h: (i, 0)),
                  pl.BlockSpec((t, wa), lambda i, h: (i, 0)),
                  pl.BlockSpec((t, yb.shape[1]), lambda i, h: (i, 0)),
                  pl.BlockSpec(w_out.shape, lambda i, h: (0, 0)),
                  pl.BlockSpec((1, d), lambda i, h: (0, 0)),
                  pl.BlockSpec((d, 2 * half), lambda i, h: (0, h)),
                  pl.BlockSpec((1, keys, half), lambda i, h: (h, 0, 0)),
                  pl.BlockSpec((1, keys, half), lambda i, h: (h, 0, 0))],
        out_specs=[pl.BlockSpec((t, d), lambda i, h: (i, 0)),
                   pl.BlockSpec((t, d), lambda i, h: (i, 0)),
                   sel_spec, sel_spec, sel_spec, sel_spec],
        out_shape=[jax.ShapeDtypeStruct((n, d), F32), jax.ShapeDtypeStruct((n, d), BF16),
                   sel_shape, sel_shape, sel_shape, sel_shape],
        scratch_shapes=[pltpu.VMEM((keys, t), F32), pltpu.VMEM((keys, t), F32)],
        compiler_params=_cparams(("parallel", "arbitrary")),
        name="peer_select",
    )(x, ya, yb, w_out.astype(BF16), g.reshape(1, d).astype(F32), wq.astype(BF16),
      k1.astype(BF16), k2.astype(BF16))


def _peer_dense_kernel(hx_ref, th_ref, e1_ref, s2_ref, e2_ref, u_ref, vt_ref, res_ref, gf_ref,
                       out_ref, acc_ref, act0_ref, act1_ref, w0_ref, w1_ref, *, final_norm):
    j = pl.program_id(1)
    n_steps = pl.num_programs(1)
    heads, n_lt, keys, _ = th_ref.shape
    t_len = n_lt * LANES
    e_sub = act0_ref.shape[0]
    n_i1 = e_sub // keys
    d_rows = acc_ref.shape[0] // n_i1
    m_rows = 2 * e_sub // n_i1
    tw = t_len // 2
    rc = 32
    c0 = math.sqrt(2.0 / math.pi)
    c1 = c0 * 0.044715

    never = lax.broadcasted_iota(jnp.int32, (rc, LANES), 0) < jnp.minimum(j, 0)

    def slot_loop(sub, do1, do2, do3):
        act_w, act_r = (act1_ref, act0_ref) if sub else (act0_ref, act1_ref)
        w_w, w_r = (w0_ref, w1_ref) if sub else (w1_ref, w0_ref)
        i1_base = (2 * j + sub - 1) * n_i1

        def body(k, carry):
            deps = []
            if do3:
                d0 = pl.multiple_of(k * d_rows, d_rows)
                p3 = _dot(vt_ref[pl.ds(d0, d_rows), sub * e_sub:(sub + 1) * e_sub], w_r[...])
                acc_ref[pl.ds(d0, d_rows), :] += p3
                deps.append([p3[c * (d_rows // n_lt):c * (d_rows // n_lt) + rc, 0:LANES] for c in range(n_lt)])
            if do1:
                m0 = pl.multiple_of((k // 2) * m_rows, m_rows)
                t0 = pl.multiple_of((k % 2) * tw, tw)
                p1 = _dot_nt(u_ref[pl.ds(sub * e_sub + m0, m_rows), :], hx_ref[pl.ds(t0, tw), :])
                act_w[pl.ds(m0, m_rows), pl.ds(t0, tw)] = p1
                deps.append([p1[c * (m_rows // n_lt):c * (m_rows // n_lt) + rc, 0:LANES] for c in range(n_lt)])
            if do2:
                i1 = i1_base + k
                r0 = pl.multiple_of(k * keys, keys)
                for lt in range(n_lt):
                    ls = slice(lt * LANES, (lt + 1) * LANES)
                    th = [jnp.broadcast_to(th_ref[h, lt, pl.ds(i1, 1), :], (rc, LANES)) for h in range(heads)]
                    e1 = [jnp.broadcast_to(e1_ref[h, lt, pl.ds(i1, 1), :], (rc, LANES)) for h in range(heads)]
                    for r in range(0, keys, rc):
                        g = jnp.zeros((rc, LANES), F32)
                        for h in range(heads):
                            g = g + jnp.where(s2_ref[h, lt, r:r + rc, :] >= th[h], e2_ref[h, lt, r:r + rc, :], 0.0) * e1[h]
                        a = act_r[pl.ds(r0 + r, rc), ls]
                        tanh = jnp.tanh(a * (c0 + c1 * (a * a)))
                        ga = g * a
                        if r == keys - rc:
                            for dep in deps:
                                ga = jnp.where(never, dep[lt], ga)
                        w_w[pl.ds(r0 + r, rc), ls] = (ga + ga * tanh).astype(BF16)
            return carry

        lax.fori_loop(0, n_i1, body, 0)

    @pl.when(j == 0)
    def _():
        acc_ref[...] = jnp.zeros_like(acc_ref)
        slot_loop(0, True, False, False)
        slot_loop(1, True, True, False)

    @pl.when(jnp.logical_and(j > 0, j < n_steps - 1))
    def _():
        slot_loop(0, True, True, True)
        slot_loop(1, True, True, True)

    @pl.when(j == n_steps - 1)
    def _():
        slot_loop(0, False, True, True)
        slot_loop(1, False, False, True)
        o = res_ref[...] + acc_ref[...].T
        if final_norm:
            o = _rms(o, gf_ref[...])
        out_ref[...] = o


def _peer_dense(hx, th, e1, s2, e2, u_tab, vt_tab, res, g_final, final_norm):
    n, d = res.shape
    heads, _, keys, _ = th.shape
    n_exp = u_tab.shape[0]
    t = min(T_PEER, n)
    e_blk = min(E_BLK, n_exp)
    nj = n_exp // e_blk
    sel_spec = pl.BlockSpec((heads, t // LANES, keys, LANES), lambda i, j: (0, i, 0, 0))
    return pl.pallas_call(
        functools.partial(_peer_dense_kernel, final_norm=final_norm),
        grid=(n // t, nj + 1),
        in_specs=[pl.BlockSpec((t, d), lambda i, j: (i, 0)),
                  sel_spec, sel_spec, sel_spec, sel_spec,
                  pl.BlockSpec((e_blk, d), lambda i, j: (jnp.minimum(j, nj - 1), 0)),
                  pl.BlockSpec((d, e_blk), lambda i, j: (0, jnp.maximum(j - 1, 0))),
                  pl.BlockSpec((t, d), lambda i, j: (i, 0)),
                  pl.BlockSpec((1, d), lambda i, j: (0, 0))],
        out_specs=pl.BlockSpec((t, d), lambda i, j: (i, 0)),
        out_shape=jax.ShapeDtypeStruct((n, d), F32),
        scratch_shapes=[pltpu.VMEM((d, t), F32),
                        pltpu.VMEM((e_blk // 2, t), F32), pltpu.VMEM((e_blk // 2, t), F32),
                        pltpu.VMEM((e_blk // 2, t), BF16), pltpu.VMEM((e_blk // 2, t), BF16)],
        compiler_params=_cparams(("parallel", "arbitrary")),
        name="peer_dense",
    )(hx, th, e1, s2, e2, u_tab, vt_tab, res, g_final.reshape(1, d).astype(F32))


def _peer_block(x, ya, yb, w_out, g_ffn, wq, k1, k2, u_tab, v_tab, g_final, final_norm):
    h, hx, th, e1, s2, e2 = _peer_select(x, ya, yb, w_out, g_ffn, wq, k1, k2)
    return _peer_dense(hx, th, e1, s2, e2, u_tab.astype(BF16), v_tab.astype(BF16).T, h, g_final,
                       final_norm)


def kernel(x, l0_norm_mix_g, l0_w_in, a_lam_re, a_lam_im, a_log_dt, a_b_re, a_b_im, a_c_re, a_c_im, a_d, a_glu_w, a_glu_b, b_conv_w, l0_w_out, l0_norm_ffn_g, l0_peer_wq, l0_peer_k1, l0_peer_k2, l0_peer_u, l0_peer_v, l1_norm_mix_g, l1_w_in, c_vnorm_g, c_ws, c_bs, l1_w_out, l1_norm_ffn_g, l1_peer_wq, l1_peer_k1, l1_peer_k2, l1_peer_u, l1_peer_v, final_norm_g):
    batch, seq, d = x.shape
    n = batch * seq
    h = x.reshape(n, d)

    (p0,) = _norm_matmul(h, l0_norm_mix_g.astype(F32), l0_w_in.astype(BF16),
                         ((0, l0_w_in.shape[1], 1.0),), (F32,))
    bq, are, aim, cq = _s5_params(a_lam_re, a_lam_im, a_log_dt, a_b_re, a_b_im, a_c_re, a_c_im)
    ya, yb = _even_mixer(p0, batch, bq, are, aim, cq, a_d, a_glu_w, a_glu_b, b_conv_w)
    h = _peer_block(h, ya, yb, l0_w_out, l0_norm_ffn_g, l0_peer_wq, l0_peer_k1, l0_peer_k2,
                    l0_peer_u, l0_peer_v, final_norm_g, False)

    cw = c_vnorm_g.shape[0]
    dw = (l1_w_in.shape[1] - 2 * cw) // 3
    o = 2 * cw
    splits = ((0, o, 1.0), (o, o + dw, SB_HEAD_DIM ** -0.5), (o + dw, o + 2 * dw, 1.0),
              (o + 2 * dw, o + 3 * dw, 1.0))
    uv, q, k, v = _norm_matmul(h, l1_norm_mix_g.astype(F32), l1_w_in.astype(BF16), splits,
                               (F32, BF16, BF16, BF16))
    yc = _gmlp(uv, c_vnorm_g, c_ws, c_bs)
    yd = _sb_attention(q, k, v, batch)
    h = _peer_block(h, yc, yd, l1_w_out, l1_norm_ffn_g, l1_peer_wq, l1_peer_k1, l1_peer_k2,
                    l1_peer_u, l1_peer_v, final_norm_g, True)
    return h.reshape(batch, seq, d)
```

```python
import functools
import math

import jax
import jax.numpy as jnp
from jax import lax
from jax.experimental import pallas as pl
from jax.experimental.pallas import tpu as pltpu

F32 = jnp.float32
BF16 = jnp.bfloat16
NEG_INF = float("-inf")

RMS_EPS = 1e-6
LANES = 128
S5_GROUP = 16
S5_STATE = 64
S5_QUARTERS = 4
PEER_HEADS = 8
PEER_KEYS = 128
PEER_TOPK = 16
SB_HEAD_DIM = 64
GMLP_CHUNK = 128
GMLP_HEAD_DIM = 64
SB_LOG_CUTOFF = -104.0

TM_PROJ = 512
T_S5 = 256
T_GMLP = 512
TQ_SB = 256
TK_SB = 128
SB_STATIC_BLOCKS = 3
T_PEER = 1024
E_BLK = 1024
VMEM_LIMIT = 58 * 1024 * 1024


def _cparams(sem, flags=None):
    return pltpu.CompilerParams(dimension_semantics=sem, vmem_limit_bytes=VMEM_LIMIT, flags=flags)


def _dot(a, b):
    return jnp.dot(a, b, preferred_element_type=F32)


def _dot_nt(a, b):
    return lax.dot_general(a, b, (((1,), (1,)), ((), ())), preferred_element_type=F32)


def _gelu(x):
    c = math.sqrt(2.0 / math.pi)
    return 0.5 * x * (1.0 + jnp.tanh(c * (x + 0.044715 * (x * x * x))))


def _rms(x, g):
    return x * lax.rsqrt(jnp.mean(x * x, axis=-1, keepdims=True) + RMS_EPS) * g


def _norm_matmul_kernel(x_ref, g_ref, w_ref, *out_refs, splits):
    xn = _rms(x_ref[...], g_ref[...]).astype(BF16)
    p = _dot(xn, w_ref[...])
    for o_ref, (lo, hi, scale) in zip(out_refs, splits):
        part = p[:, lo:hi]
        if scale != 1.0:
            part = part * scale
        o_ref[...] = part.astype(o_ref.dtype)


def _norm_matmul(x, g, w, splits, dtypes):
    n, d = x.shape
    m = w.shape[1]
    tm = min(TM_PROJ, n)
    out_shape = [jax.ShapeDtypeStruct((n, hi - lo), dt) for (lo, hi, _), dt in zip(splits, dtypes)]
    out_specs = [pl.BlockSpec((tm, hi - lo), lambda i: (i, 0)) for (lo, hi, _) in splits]
    return pl.pallas_call(
        functools.partial(_norm_matmul_kernel, splits=splits),
        grid=(n // tm,),
        in_specs=[pl.BlockSpec((tm, d), lambda i: (i, 0)),
                  pl.BlockSpec((1, d), lambda i: (0, 0)),
                  pl.BlockSpec((d, m), lambda i: (0, 0))],
        out_specs=out_specs,
        out_shape=out_shape,
        compiler_params=_cparams(("parallel",)),
        name="norm_matmul",
    )(x, g.reshape(1, d), w)


def _even_mixer_kernel(p_ref, bq_ref, are_ref, aim_ref, cq_ref, d_ref, gw_ref, gb_ref, cw_ref,
                       ya_ref, yb_ref, bu_ref, hst_ref, zs_ref):
    t_len = p_ref.shape[0]
    aw = ya_ref.shape[1]
    qs = 2 * aw
    half = qs // 2

    @pl.when(pl.program_id(1) == 0)
    def _():
        hst_ref[...] = jnp.zeros_like(hst_ref)
        zs_ref[0:8, :] = jnp.zeros((8, zs_ref.shape[1]), F32)

    u = p_ref[:, 0:aw]
    ub = u.astype(BF16)
    for q in range(S5_QUARTERS):
        bu_ref[:, q * qs:(q + 1) * qs] = _dot(ub[:, q * LANES:(q + 1) * LANES], bq_ref[q])

    for q in range(S5_QUARTERS):
        ar = are_ref[q:q + 1, :]
        ai = aim_ref[q:q + 1, :]
        re_lo = q * qs
        im_lo = q * qs + half

        def step(t, carry, ar=ar, ai=ai, re_lo=re_lo, im_lo=im_lo):
            hr, hi = carry
            br = bu_ref[pl.ds(t, 1), re_lo:re_lo + half]
            bi = bu_ref[pl.ds(t, 1), im_lo:im_lo + half]
            nr = ar * hr - ai * hi + br
            ni = ar * hi + ai * hr + bi
            bu_ref[pl.ds(t, 1), re_lo:re_lo + half] = nr
            bu_ref[pl.ds(t, 1), im_lo:im_lo + half] = ni
            return nr, ni

        hr, hi = lax.fori_loop(0, t_len, step,
                               (hst_ref[2 * q:2 * q + 1, :], hst_ref[2 * q + 1:2 * q + 2, :]), unroll=8)
        hst_ref[2 * q:2 * q + 1, :] = hr
        hst_ref[2 * q + 1:2 * q + 2, :] = hi

    ys = [_dot(bu_ref[:, q * qs:(q + 1) * qs].astype(BF16), cq_ref[q]) for q in range(S5_QUARTERS)]
    y = jnp.concatenate(ys, axis=1) + d_ref[...] * u
    y = _gelu(y)
    gate = jax.nn.sigmoid(_dot(y.astype(BF16), gw_ref[...]) + gb_ref[...])
    ya_ref[...] = (y * gate).astype(ya_ref.dtype)

    bw = yb_ref.shape[1]
    z = p_ref[:, aw + bw:aw + 2 * bw] * p_ref[:, aw + 2 * bw:aw + 3 * bw]
    zs_ref[8:8 + t_len, :] = z
    conv = (cw_ref[2:3, :] * z + cw_ref[1:2, :] * zs_ref[7:7 + t_len, :]
            + cw_ref[0:1, :] * zs_ref[6:6 + t_len, :])
    yb_ref[...] = (p_ref[:, aw:aw + bw] * conv).astype(yb_ref.dtype)
    zs_ref[0:8, :] = zs_ref[t_len:t_len + 8, :]


def _s5_params(lam_re, lam_im, log_dt, b_re, b_im, c_re, c_im):
    g, p = lam_re.shape
    gq = g // S5_QUARTERS
    dt = jnp.exp(log_dt.astype(F32))[:, None]
    lr = lam_re.astype(F32)
    li = lam_im.astype(F32)
    mag = jnp.exp(lr * dt)
    ar = mag * jnp.cos(li * dt)
    ai = mag * jnp.sin(li * dt)
    den = lr * lr + li * li
    nr = ar - 1.0
    fr = (nr * lr + ai * li) / den
    fi = (ai * lr - nr * li) / den
    bre = b_re.astype(F32)
    bim = b_im.astype(F32)
    bbar_re = fr[..., None] * bre - fi[..., None] * bim
    bbar_im = fr[..., None] * bim + fi[..., None] * bre
    eye = jnp.eye(gq, dtype=F32)

    def in_blockdiag(m):
        m = m.reshape(S5_QUARTERS, gq, p, S5_GROUP)
        return jnp.einsum('qgpc,gh->qgchp', m, eye).reshape(S5_QUARTERS, gq * S5_GROUP, gq * p)

    def out_blockdiag(m):
        m = m.reshape(S5_QUARTERS, gq, S5_GROUP, p)
        return jnp.einsum('qgcp,gh->qgphc', m, eye).reshape(S5_QUARTERS, gq * p, gq * S5_GROUP)

    bq = jnp.concatenate([in_blockdiag(bbar_re), in_blockdiag(bbar_im)], axis=2).astype(BF16)
    cq = jnp.concatenate([out_blockdiag(c_re.astype(F32)), -out_blockdiag(c_im.astype(F32))],
                         axis=1).astype(BF16)
    are = ar.reshape(S5_QUARTERS, gq * p)
    aim = ai.reshape(S5_QUARTERS, gq * p)
    return bq, are, aim, cq


def _even_mixer(p, batch, bq, are, aim, cq, d_skip, glu_w, glu_b, conv_w):
    n, width = p.shape
    seq = n // batch
    aw = d_skip.size
    bw = conv_w.shape[1]
    t = min(T_S5, seq)
    nt = seq // t
    qs = 2 * aw
    full = lambda shape: pl.BlockSpec(shape, lambda b, i: (0,) * len(shape))
    return pl.pallas_call(
        _even_mixer_kernel,
        grid=(batch, nt),
        in_specs=[pl.BlockSpec((t, width), lambda b, i: (b * nt + i, 0)),
                  full(bq.shape), full(are.shape), full(aim.shape), full(cq.shape),
                  full((1, aw)), full(glu_w.shape), full((1, aw)), full(conv_w.shape)],
        out_specs=[pl.BlockSpec((t, aw), lambda b, i: (b * nt + i, 0)),
                   pl.BlockSpec((t, bw), lambda b, i: (b * nt + i, 0))],
        out_shape=[jax.ShapeDtypeStruct((n, aw), BF16), jax.ShapeDtypeStruct((n, bw), BF16)],
        scratch_shapes=[pltpu.VMEM((t, S5_QUARTERS * qs), F32),
                        pltpu.VMEM((2 * S5_QUARTERS, aw), F32),
                        pltpu.VMEM((t + 8, bw), F32)],
        compiler_params=_cparams(("arbitrary", "arbitrary")),
        name="even_mixer",
    )(p, bq, are, aim, cq, d_skip.reshape(1, aw).astype(F32), glu_w.astype(BF16),
      glu_b.reshape(1, aw).astype(F32), conv_w.astype(F32))


def _gmlp_kernel(uv_ref, g_ref, wm_ref, bias_ref, y_ref):
    cw = y_ref.shape[1]
    t_len = y_ref.shape[0]
    u = _gelu(uv_ref[:, 0:cw])
    v = _rms(_gelu(uv_ref[:, cw:2 * cw]), g_ref[...]).astype(BF16)
    lane = lax.broadcasted_iota(jnp.int32, (GMLP_CHUNK, LANES), 1)
    first_head = lane < GMLP_HEAD_DIM
    for c in range(t_len // GMLP_CHUNK):
        r0 = c * GMLP_CHUNK
        tiles = []
        for j in range(cw // LANES):
            vc = v[r0:r0 + GMLP_CHUNK, j * LANES:(j + 1) * LANES]
            tiles.append(jnp.where(first_head, _dot(wm_ref[2 * j], vc), _dot(wm_ref[2 * j + 1], vc)))
        y = jnp.concatenate(tiles, axis=1) + bias_ref[...]
        y_ref[r0:r0 + GMLP_CHUNK, :] = (u[r0:r0 + GMLP_CHUNK, :] * y).astype(y_ref.dtype)


def _gmlp(uv, vnorm_g, ws, bs):
    n = uv.shape[0]
    cw = uv.shape[1] // 2
    heads = ws.shape[0]
    t = min(T_GMLP, n)
    tril = jnp.tril(jnp.ones((GMLP_CHUNK, GMLP_CHUNK), dtype=bool))
    wm = jnp.where(tril, ws, jnp.zeros_like(ws)).astype(BF16)
    bias = jnp.repeat(bs.T.astype(F32), cw // heads, axis=1)
    return pl.pallas_call(
        _gmlp_kernel,
        grid=(n // t,),
        in_specs=[pl.BlockSpec((t, 2 * cw), lambda i: (i, 0)),
                  pl.BlockSpec((1, cw), lambda i: (0, 0)),
                  pl.BlockSpec(wm.shape, lambda i: (0, 0, 0)),
                  pl.BlockSpec(bias.shape, lambda i: (0, 0))],
        out_specs=pl.BlockSpec((t, cw), lambda i: (i, 0)),
        out_shape=jax.ShapeDtypeStruct((n, cw), BF16),
        compiler_params=_cparams(("parallel",)),
        name="gmlp",
    )(uv, vnorm_g.reshape(1, cw).astype(F32), wm, bias)


def _sb_attn_kernel(q_ref, k_ref, v_ref, tri_ref, o_ref, acc_ref, run_ref):
    qi = pl.program_id(2)
    n_sub = q_ref.shape[0] // TK_SB
    q = q_ref[...]
    lane = lax.broadcasted_iota(jnp.int32, q.shape, 1)
    lane_sub = lax.broadcasted_iota(jnp.int32, (TK_SB, LANES), 1)
    q_h0 = jnp.where(lane < SB_HEAD_DIM, q, jnp.zeros_like(q))
    q_h1 = jnp.where(lane < SB_HEAD_DIM, jnp.zeros_like(q), q)
    row = lax.broadcasted_iota(jnp.int32, (2 * TK_SB, TK_SB), 0)
    col = lax.broadcasted_iota(jnp.int32, (2 * TK_SB, TK_SB), 1)
    below_diag = col < jnp.where(row >= TK_SB, row - TK_SB, row)

    def visit(qs, kb, run, acc, diag):
        okf = jnp.where(kb >= 0, 1.0, 0.0).astype(F32)
        k0 = pl.multiple_of(jnp.maximum(kb, 0) * TK_SB, TK_SB)
        kblk = k_ref[pl.ds(k0, TK_SB), :]
        vblk = v_ref[pl.ds(k0, TK_SB), :]
        z = _dot_nt(qs, kblk)
        sp = jnp.maximum(z, 0.0) + jnp.log(1.0 + jnp.exp(-jnp.abs(z)))
        lm = jnp.where(below_diag, -sp, 0.0) if diag else -sp * okf
        lm_hi = lm.astype(BF16)
        lm_lo = (lm - lm_hi.astype(F32)).astype(BF16)
        cs = _dot(jnp.concatenate([lm_hi, lm_lo], axis=0), tri_ref[...])
        cs = cs[:2 * TK_SB] + cs[2 * TK_SB:]
        w = jnp.exp(z - sp + cs[:, :TK_SB] + run)
        w = jnp.where(below_diag, w, 0.0) if diag else w * okf
        return run + cs[:, TK_SB:], acc + _dot(w.astype(BF16), vblk)

    def stacked_q(sub):
        rows = slice(sub * TK_SB, (sub + 1) * TK_SB)
        return jnp.concatenate([q_h0[rows], q_h1[rows]], axis=0)

    live = False
    for sub in range(n_sub):
        qs = stacked_q(sub)
        run = jnp.zeros((2 * TK_SB, TK_SB), F32)
        acc = jnp.zeros((2 * TK_SB, LANES), F32)
        for o in range(SB_STATIC_BLOCKS):
            run, acc = visit(qs, qi * n_sub + sub - o, run, acc, o == 0)
        run_ref[sub] = run
        acc_ref[sub] = acc
        live = jnp.logical_or(live, jnp.max(run) > SB_LOG_CUTOFF)

    def cond(c):
        o, live = c
        return jnp.logical_and(live, o <= qi * n_sub + n_sub - 1)

    def body(c):
        o, _ = c
        live = False
        for sub in range(n_sub):
            run, acc = visit(stacked_q(sub), qi * n_sub + sub - o, run_ref[sub], acc_ref[sub], False)
            run_ref[sub] = run
            acc_ref[sub] = acc
            live = jnp.logical_or(live, jnp.max(run) > SB_LOG_CUTOFF)
        return o + 1, live

    lax.while_loop(cond, body, (SB_STATIC_BLOCKS, live))
    for sub in range(n_sub):
        acc = acc_ref[sub]
        rows = slice(sub * TK_SB, (sub + 1) * TK_SB)
        o_ref[rows, :] = jnp.where(lane_sub < SB_HEAD_DIM, acc[:TK_SB], acc[TK_SB:]).astype(o_ref.dtype)


def _sb_attention(q, k, v, batch):
    n, dw = q.shape
    seq = n // batch
    tq = min(TQ_SB, seq)
    nq = seq // tq
    npair = dw // LANES
    s_idx = jnp.arange(TK_SB)[:, None]
    j_idx = jnp.arange(TK_SB)[None, :]
    tri = jnp.concatenate([(s_idx > j_idx), jnp.ones((TK_SB, TK_SB), bool)], axis=1).astype(BF16)
    return pl.pallas_call(
        _sb_attn_kernel,
        grid=(batch, npair, nq),
        in_specs=[pl.BlockSpec((tq, LANES), lambda b, h, i: (b * nq + i, h)),
                  pl.BlockSpec((seq, LANES), lambda b, h, i: (b, h)),
                  pl.BlockSpec((seq, LANES), lambda b, h, i: (b, h)),
                  pl.BlockSpec(tri.shape, lambda b, h, i: (0, 0))],
        out_specs=pl.BlockSpec((tq, LANES), lambda b, h, i: (b * nq + i, h)),
        out_shape=jax.ShapeDtypeStruct((n, dw), BF16),
        scratch_shapes=[pltpu.VMEM((tq // TK_SB, 2 * TK_SB, LANES), F32),
                        pltpu.VMEM((tq // TK_SB, 2 * TK_SB, TK_SB), F32)],
        compiler_params=_cparams(("parallel", "parallel", "arbitrary")),
        name="sb_attn",
    )(q, k, v, tri)


def _top_values(s, n, rows_out):
    riota = lax.broadcasted_iota(jnp.int32, (rows_out, s.shape[1]), 0)
    vals = jnp.full((rows_out, s.shape[1]), NEG_INF, F32)
    cur = s
    for i in range(n):
        m = jnp.max(cur, axis=0, keepdims=True)
        vals = jnp.where(riota == i, m, vals)
        cur = jnp.where(cur == m, NEG_INF, cur)
    return vals


def _peer_select_kernel(x_ref, ya_ref, yb_ref, wo_ref, g_ref, wq_ref, k1_ref, k2_ref,
                        h_ref, hx_ref, th_ref, e1_ref, s2_ref, e2_ref, s1_ref, s2s_ref):
    t_len = x_ref.shape[0]
    wa = ya_ref.shape[1]

    @pl.when(pl.program_id(1) == 0)
    def _():
        mix = _dot(ya_ref[...], wo_ref[0:wa, :]) + _dot(yb_ref[...], wo_ref[wa:, :])
        h = x_ref[...] + mix
        h_ref[...] = h
        hx_ref[...] = _rms(h, g_ref[...]).astype(BF16)

    q = _dot(hx_ref[...], wq_ref[...])
    half = q.shape[1] // 2
    s1_ref[...] = _dot_nt(k1_ref[0], q[:, :half].astype(BF16))
    s2s_ref[...] = _dot_nt(k2_ref[0], q[:, half:].astype(BF16))

    n = PEER_TOPK + 1
    rows = 24
    i8 = lax.broadcasted_iota(jnp.int32, (8, LANES), 0)

    def chunk(c, carry):
        l0 = pl.multiple_of(c * LANES, LANES)
        s1 = s1_ref[:, pl.ds(l0, LANES)]
        s2 = s2s_ref[:, pl.ds(l0, LANES)]
        v1 = _top_values(s1, n, rows)
        v2 = _top_values(s2, n, rows)
        cands = [v1[0:1] + v2, v1[1:2] + v2[0:8]]
        for a, nb in ((2, 5), (3, 4), (4, 3), (5, 2), (6, 2), (7, 2)):
            cands.append(jnp.where(i8 < nb, v1[a:a + 1] + v2[0:8], NEG_INF))
        cands.append(v1[8:24] + v2[0:1])
        cand = jnp.concatenate(cands, axis=0)
        top = _top_values(cand, n, rows)
        t16 = top[PEER_TOPK - 1:PEER_TOPK]
        t17 = top[PEER_TOPK:PEER_TOPK + 1]
        tau = jnp.where(t17 == NEG_INF, t16, 0.5 * (t16 + t17))
        m1 = v1[0:1]
        m2 = v2[0:1]
        zsum = jnp.sum(jnp.where(cand >= tau, jnp.exp(cand - (m1 + m2)), 0.0), axis=0, keepdims=True)
        th_ref[0, c] = tau - s1
        e1_ref[0, c] = jnp.exp(s1 - m1)
        s2_ref[0, c] = s2
        e2_ref[0, c] = jnp.exp(s2 - m2) * (0.5 / zsum)
        return carry

    lax.fori_loop(0, t_len // LANES, chunk, 0, unroll=2)


def _peer_select(x, ya, yb, w_out, g, wq, k1, k2):
    n, d = x.shape
    wa = ya.shape[1]
    heads, keys, half = k1.shape
    t = min(T_PEER, n)
    sel_shape = jax.ShapeDtypeStruct((heads, n // LANES, keys, LANES), F32)
    sel_spec = pl.BlockSpec((1, t // LANES, keys, LANES), lambda i, h: (h, i, 0, 0))
    return pl.pallas_call(
        _peer_select_kernel,
        grid=(n // t, heads),
        in_specs=[pl.BlockSpec((t, d), lambda i, h: (i, 0)),
                  pl.BlockSpec((t, wa), lambda i, h: (i, 0)),
                  pl.BlockSpec((t, yb.shape[1]), lambda i, h: (i, 0)),
                  pl.BlockSpec(w_out.shape, lambda i, h: (0, 0)),
                  pl.BlockSpec((1, d), lambda i, h: (0, 0)),
                  pl.BlockSpec((d, 2 * half), lambda i, h: (0, h)),
                  pl.BlockSpec((1, keys, half), lambda i, h: (h, 0, 0)),
                  pl.BlockSpec((1, keys, half), lambda i, h: (h, 0, 0))],
        out_specs=[pl.BlockSpec((t, d), lambda i, h: (i, 0)),
                   pl.BlockSpec((t, d), lambda i, h: (i, 0)),
                   sel_spec, sel_spec, sel_spec, sel_spec],
        out_shape=[jax.ShapeDtypeStruct((n, d), F32), jax.ShapeDtypeStruct((n, d), BF16),
                   sel_shape, sel_shape, sel_shape, sel_shape],
        scratch_shapes=[pltpu.VMEM((keys, t), F32), pltpu.VMEM((keys, t), F32)],
        compiler_params=_cparams(("parallel", "arbitrary")),
        name="peer_select",
    )(x, ya, yb, w_out.astype(BF16), g.reshape(1, d).astype(F32), wq.astype(BF16),
      k1.astype(BF16), k2.astype(BF16))


def _peer_dense_kernel(hx_ref, th_ref, e1_ref, s2_ref, e2_ref, u_ref, vt_ref, res_ref, gf_ref,
                       out_ref, acc_ref, act0_ref, act1_ref, w0_ref, w1_ref, *, final_norm):
    j = pl.program_id(1)
    n_steps = pl.num_programs(1)
    heads, n_lt, keys, _ = th_ref.shape
    t_len = n_lt * LANES
    e_sub = act0_ref.shape[0]
    n_i1 = e_sub // keys
    d_rows = acc_ref.shape[0] // n_i1
    m_rows = 2 * e_sub // n_i1
    tw = t_len // 2
    rc = 32
    c0 = math.sqrt(2.0 / math.pi)
    c1 = c0 * 0.044715

    never = lax.broadcasted_iota(jnp.int32, (rc, LANES), 0) < jnp.minimum(j, 0)

    def slot_loop(sub, do1, do2, do3):
        act_w, act_r = (act1_ref, act0_ref) if sub else (act0_ref, act1_ref)
        w_w, w_r = (w0_ref, w1_ref) if sub else (w1_ref, w0_ref)
        i1_base = (2 * j + sub - 1) * n_i1

        def body(k, carry):
            deps = []
            if do3:
                d0 = pl.multiple_of(k * d_rows, d_rows)
                p3 = _dot(vt_ref[pl.ds(d0, d_rows), sub * e_sub:(sub + 1) * e_sub], w_r[...])
                acc_ref[pl.ds(d0, d_rows), :] += p3
                deps.append([p3[c * (d_rows // n_lt):c * (d_rows // n_lt) + rc, 0:LANES] for c in range(n_lt)])
            if do1:
                m0 = pl.multiple_of((k // 2) * m_rows, m_rows)
                t0 = pl.multiple_of((k % 2) * tw, tw)
                p1 = _dot_nt(u_ref[pl.ds(sub * e_sub + m0, m_rows), :], hx_ref[pl.ds(t0, tw), :])
                act_w[pl.ds(m0, m_rows), pl.ds(t0, tw)] = p1
                deps.append([p1[c * (m_rows // n_lt):c * (m_rows // n_lt) + rc, 0:LANES] for c in range(n_lt)])
            if do2:
                i1 = i1_base + k
                r0 = pl.multiple_of(k * keys, keys)
                for lt in range(n_lt):
                    ls = slice(lt * LANES, (lt + 1) * LANES)
                    th = [jnp.broadcast_to(th_ref[h, lt, pl.ds(i1, 1), :], (rc, LANES)) for h in range(heads)]
                    e1 = [jnp.broadcast_to(e1_ref[h, lt, pl.ds(i1, 1), :], (rc, LANES)) for h in range(heads)]
                    for r in range(0, keys, rc):
                        g = jnp.zeros((rc, LANES), F32)
                        for h in range(heads):
                            g = g + jnp.where(s2_ref[h, lt, r:r + rc, :] >= th[h], e2_ref[h, lt, r:r + rc, :], 0.0) * e1[h]
                        a = act_r[pl.ds(r0 + r, rc), ls]
                        tanh = jnp.tanh(a * (c0 + c1 * (a * a)))
                        ga = g * a
                        if r == keys - rc:
                            for dep in deps:
                                ga = jnp.where(never, dep[lt], ga)
                        w_w[pl.ds(r0 + r, rc), ls] = (ga + ga * tanh).astype(BF16)
            return carry

        lax.fori_loop(0, n_i1, body, 0)

    @pl.when(j == 0)
    def _():
        acc_ref[...] = jnp.zeros_like(acc_ref)
        slot_loop(0, True, False, False)
        slot_loop(1, True, True, False)

    @pl.when(jnp.logical_and(j > 0, j < n_steps - 1))
    def _():
        slot_loop(0, True, True, True)
        slot_loop(1, True, True, True)

    @pl.when(j == n_steps - 1)
    def _():
        slot_loop(0, False, True, True)
        slot_loop(1, False, False, True)
        o = res_ref[...] + acc_ref[...].T
        if final_norm:
            o = _rms(o, gf_ref[...])
        out_ref[...] = o


def _peer_dense(hx, th, e1, s2, e2, u_tab, vt_tab, res, g_final, final_norm):
    n, d = res.shape
    heads, _, keys, _ = th.shape
    n_exp = u_tab.shape[0]
    t = min(T_PEER, n)
    e_blk = min(E_BLK, n_exp)
    nj = n_exp // e_blk
    once = pl.Buffered(1)
    sel_spec = pl.BlockSpec((heads, t // LANES, keys, LANES), lambda i, j: (0, i, 0, 0), pipeline_mode=once)
    return pl.pallas_call(
        functools.partial(_peer_dense_kernel, final_norm=final_norm),
        grid=(n // t, nj + 1),
        in_specs=[pl.BlockSpec((t, d), lambda i, j: (i, 0), pipeline_mode=once),
                  sel_spec, sel_spec, sel_spec, sel_spec,
                  pl.BlockSpec((e_blk, d), lambda i, j: (jnp.minimum(j, nj - 1), 0)),
                  pl.BlockSpec((d, e_blk), lambda i, j: (0, jnp.maximum(j - 1, 0))),
                  pl.BlockSpec((t, d), lambda i, j: (i, 0), pipeline_mode=once),
                  pl.BlockSpec((1, d), lambda i, j: (0, 0))],
        out_specs=pl.BlockSpec((t, d), lambda i, j: (i, 0)),
        out_shape=jax.ShapeDtypeStruct((n, d), F32),
        scratch_shapes=[pltpu.VMEM((d, t), F32),
                        pltpu.VMEM((e_blk // 2, t), F32), pltpu.VMEM((e_blk // 2, t), F32),
                        pltpu.VMEM((e_blk // 2, t), BF16), pltpu.VMEM((e_blk // 2, t), BF16)],
        compiler_params=_cparams(("parallel", "arbitrary")),
        name="peer_dense",
    )(hx, th, e1, s2, e2, u_tab, vt_tab, res, g_final.reshape(1, d).astype(F32))


def _peer_block(x, ya, yb, w_out, g_ffn, wq, k1, k2, u_tab, v_tab, g_final, final_norm):
    h, hx, th, e1, s2, e2 = _peer_select(x, ya, yb, w_out, g_ffn, wq, k1, k2)
    return _peer_dense(hx, th, e1, s2, e2, u_tab.astype(BF16), v_tab.astype(BF16).T, h, g_final,
                       final_norm)


def kernel(x, l0_norm_mix_g, l0_w_in, a_lam_re, a_lam_im, a_log_dt, a_b_re, a_b_im, a_c_re, a_c_im, a_d, a_glu_w, a_glu_b, b_conv_w, l0_w_out, l0_norm_ffn_g, l0_peer_wq, l0_peer_k1, l0_peer_k2, l0_peer_u, l0_peer_v, l1_norm_mix_g, l1_w_in, c_vnorm_g, c_ws, c_bs, l1_w_out, l1_norm_ffn_g, l1_peer_wq, l1_peer_k1, l1_peer_k2, l1_peer_u, l1_peer_v, final_norm_g):
    batch, seq, d = x.shape
    n = batch * seq
    h = x.reshape(n, d)

    (p0,) = _norm_matmul(h, l0_norm_mix_g.astype(F32), l0_w_in.astype(BF16),
                         ((0, l0_w_in.shape[1], 1.0),), (F32,))
    bq, are, aim, cq = _s5_params(a_lam_re, a_lam_im, a_log_dt, a_b_re, a_b_im, a_c_re, a_c_im)
    ya, yb = _even_mixer(p0, batch, bq, are, aim, cq, a_d, a_glu_w, a_glu_b, b_conv_w)
    h = _peer_block(h, ya, yb, l0_w_out, l0_norm_ffn_g, l0_peer_wq, l0_peer_k1, l0_peer_k2,
                    l0_peer_u, l0_peer_v, final_norm_g, False)

    cw = c_vnorm_g.shape[0]
    dw = (l1_w_in.shape[1] - 2 * cw) // 3
    o = 2 * cw
    splits = ((0, o, 1.0), (o, o + dw, SB_HEAD_DIM ** -0.5), (o + dw, o + 2 * dw, 1.0),
              (o + 2 * dw, o + 3 * dw, 1.0))
    uv, q, k, v = _norm_matmul(h, l1_norm_mix_g.astype(F32), l1_w_in.astype(BF16), splits,
                               (F32, BF16, BF16, BF16))
    yc = _gmlp(uv, c_vnorm_g, c_ws, c_bs)
    yd = _sb_attention(q, k, v, batch)
    h = _peer_block(h, yc, yd, l1_w_out, l1_norm_ffn_g, l1_peer_wq, l1_peer_k1, l1_peer_k2,
                    l1_peer_u, l1_peer_v, final_norm_g, True)
    return h.reshape(batch, seq, d)
```

```python
import functools
import math

import jax
import jax.numpy as jnp
from jax import lax
from jax.experimental import pallas as pl
from jax.experimental.pallas import tpu as pltpu

F32 = jnp.float32
BF16 = jnp.bfloat16
NEG_INF = float("-inf")

RMS_EPS = 1e-6
LANES = 128
S5_GROUP = 16
S5_STATE = 64
S5_QUARTERS = 4
PEER_HEADS = 8
PEER_KEYS = 128
PEER_TOPK = 16
SB_HEAD_DIM = 64
GMLP_CHUNK = 128
GMLP_HEAD_DIM = 64
SB_LOG_CUTOFF = -104.0

TM_PROJ = 512
T_S5 = 256
T_GMLP = 512
TQ_SB = 256
TK_SB = 128
SB_STATIC_BLOCKS = 3
T_PEER = 1024
E_BLK = 1024
VMEM_LIMIT = 58 * 1024 * 1024


def _cparams(sem, flags=None):
    return pltpu.CompilerParams(dimension_semantics=sem, vmem_limit_bytes=VMEM_LIMIT, flags=flags)


def _dot(a, b):
    return jnp.dot(a, b, preferred_element_type=F32)


def _dot_nt(a, b):
    return lax.dot_general(a, b, (((1,), (1,)), ((), ())), preferred_element_type=F32)


def _gelu(x):
    c = math.sqrt(2.0 / math.pi)
    return 0.5 * x * (1.0 + jnp.tanh(c * (x + 0.044715 * (x * x * x))))


def _rms(x, g):
    return x * lax.rsqrt(jnp.mean(x * x, axis=-1, keepdims=True) + RMS_EPS) * g


def _norm_matmul_kernel(x_ref, g_ref, w_ref, *out_refs, splits):
    xn = _rms(x_ref[...], g_ref[...]).astype(BF16)
    p = _dot(xn, w_ref[...])
    for o_ref, (lo, hi, scale) in zip(out_refs, splits):
        part = p[:, lo:hi]
        if scale != 1.0:
            part = part * scale
        o_ref[...] = part.astype(o_ref.dtype)


def _norm_matmul(x, g, w, splits, dtypes):
    n, d = x.shape
    m = w.shape[1]
    tm = min(TM_PROJ, n)
    out_shape = [jax.ShapeDtypeStruct((n, hi - lo), dt) for (lo, hi, _), dt in zip(splits, dtypes)]
    out_specs = [pl.BlockSpec((tm, hi - lo), lambda i: (i, 0)) for (lo, hi, _) in splits]
    return pl.pallas_call(
        functools.partial(_norm_matmul_kernel, splits=splits),
        grid=(n // tm,),
        in_specs=[pl.BlockSpec((tm, d), lambda i: (i, 0)),
                  pl.BlockSpec((1, d), lambda i: (0, 0)),
                  pl.BlockSpec((d, m), lambda i: (0, 0))],
        out_specs=out_specs,
        out_shape=out_shape,
        compiler_params=_cparams(("parallel",)),
        name="norm_matmul",
    )(x, g.reshape(1, d), w)


def _even_mixer_kernel(p_ref, bq_ref, are_ref, aim_ref, cq_ref, d_ref, gw_ref, gb_ref, cw_ref,
                       ya_ref, yb_ref, bu_ref, hst_ref, zs_ref):
    t_len = p_ref.shape[0]
    aw = ya_ref.shape[1]
    qs = 2 * aw
    half = qs // 2

    @pl.when(pl.program_id(1) == 0)
    def _():
        hst_ref[...] = jnp.zeros_like(hst_ref)
        zs_ref[0:8, :] = jnp.zeros((8, zs_ref.shape[1]), F32)

    u = p_ref[:, 0:aw]
    ub = u.astype(BF16)
    for q in range(S5_QUARTERS):
        bu_ref[:, q * qs:(q + 1) * qs] = _dot(ub[:, q * LANES:(q + 1) * LANES], bq_ref[q])

    for q in range(S5_QUARTERS):
        ar = are_ref[q:q + 1, :]
        ai = aim_ref[q:q + 1, :]
        re_lo = q * qs
        im_lo = q * qs + half

        def step(t, carry, ar=ar, ai=ai, re_lo=re_lo, im_lo=im_lo):
            hr, hi = carry
            br = bu_ref[pl.ds(t, 1), re_lo:re_lo + half]
            bi = bu_ref[pl.ds(t, 1), im_lo:im_lo + half]
            nr = ar * hr - ai * hi + br
            ni = ar * hi + ai * hr + bi
            bu_ref[pl.ds(t, 1), re_lo:re_lo + half] = nr
            bu_ref[pl.ds(t, 1), im_lo:im_lo + half] = ni
            return nr, ni

        hr, hi = lax.fori_loop(0, t_len, step,
                               (hst_ref[2 * q:2 * q + 1, :], hst_ref[2 * q + 1:2 * q + 2, :]), unroll=8)
        hst_ref[2 * q:2 * q + 1, :] = hr
        hst_ref[2 * q + 1:2 * q + 2, :] = hi

    ys = [_dot(bu_ref[:, q * qs:(q + 1) * qs].astype(BF16), cq_ref[q]) for q in range(S5_QUARTERS)]
    y = jnp.concatenate(ys, axis=1) + d_ref[...] * u
    y = _gelu(y)
    gate = jax.nn.sigmoid(_dot(y.astype(BF16), gw_ref[...]) + gb_ref[...])
    ya_ref[...] = (y * gate).astype(ya_ref.dtype)

    bw = yb_ref.shape[1]
    z = p_ref[:, aw + bw:aw + 2 * bw] * p_ref[:, aw + 2 * bw:aw + 3 * bw]
    zs_ref[8:8 + t_len, :] = z
    conv = (cw_ref[2:3, :] * z + cw_ref[1:2, :] * zs_ref[7:7 + t_len, :]
            + cw_ref[0:1, :] * zs_ref[6:6 + t_len, :])
    yb_ref[...] = (p_ref[:, aw:aw + bw] * conv).astype(yb_ref.dtype)
    zs_ref[0:8, :] = zs_ref[t_len:t_len + 8, :]


def _s5_params(lam_re, lam_im, log_dt, b_re, b_im, c_re, c_im):
    g, p = lam_re.shape
    gq = g // S5_QUARTERS
    dt = jnp.exp(log_dt.astype(F32))[:, None]
    lr = lam_re.astype(F32)
    li = lam_im.astype(F32)
    mag = jnp.exp(lr * dt)
    ar = mag * jnp.cos(li * dt)
    ai = mag * jnp.sin(li * dt)
    den = lr * lr + li * li
    nr = ar - 1.0
    fr = (nr * lr + ai * li) / den
    fi = (ai * lr - nr * li) / den
    bre = b_re.astype(F32)
    bim = b_im.astype(F32)
    bbar_re = fr[..., None] * bre - fi[..., None] * bim
    bbar_im = fr[..., None] * bim + fi[..., None] * bre
    eye = jnp.eye(gq, dtype=F32)

    def in_blockdiag(m):
        m = m.reshape(S5_QUARTERS, gq, p, S5_GROUP)
        return jnp.einsum('qgpc,gh->qgchp', m, eye).reshape(S5_QUARTERS, gq * S5_GROUP, gq * p)

    def out_blockdiag(m):
        m = m.reshape(S5_QUARTERS, gq, S5_GROUP, p)
        return jnp.einsum('qgcp,gh->qgphc', m, eye).reshape(S5_QUARTERS, gq * p, gq * S5_GROUP)

    bq = jnp.concatenate([in_blockdiag(bbar_re), in_blockdiag(bbar_im)], axis=2).astype(BF16)
    cq = jnp.concatenate([out_blockdiag(c_re.astype(F32)), -out_blockdiag(c_im.astype(F32))],
                         axis=1).astype(BF16)
    are = ar.reshape(S5_QUARTERS, gq * p)
    aim = ai.reshape(S5_QUARTERS, gq * p)
    return bq, are, aim, cq


def _even_mixer(p, batch, bq, are, aim, cq, d_skip, glu_w, glu_b, conv_w):
    n, width = p.shape
    seq = n // batch
    aw = d_skip.size
    bw = conv_w.shape[1]
    t = min(T_S5, seq)
    nt = seq // t
    qs = 2 * aw
    full = lambda shape: pl.BlockSpec(shape, lambda b, i: (0,) * len(shape))
    return pl.pallas_call(
        _even_mixer_kernel,
        grid=(batch, nt),
        in_specs=[pl.BlockSpec((t, width), lambda b, i: (b * nt + i, 0)),
                  full(bq.shape), full(are.shape), full(aim.shape), full(cq.shape),
                  full((1, aw)), full(glu_w.shape), full((1, aw)), full(conv_w.shape)],
        out_specs=[pl.BlockSpec((t, aw), lambda b, i: (b * nt + i, 0)),
                   pl.BlockSpec((t, bw), lambda b, i: (b * nt + i, 0))],
        out_shape=[jax.ShapeDtypeStruct((n, aw), BF16), jax.ShapeDtypeStruct((n, bw), BF16)],
        scratch_shapes=[pltpu.VMEM((t, S5_QUARTERS * qs), F32),
                        pltpu.VMEM((2 * S5_QUARTERS, aw), F32),
                        pltpu.VMEM((t + 8, bw), F32)],
        compiler_params=_cparams(("arbitrary", "arbitrary")),
        name="even_mixer",
    )(p, bq, are, aim, cq, d_skip.reshape(1, aw).astype(F32), glu_w.astype(BF16),
      glu_b.reshape(1, aw).astype(F32), conv_w.astype(F32))


def _gmlp_kernel(uv_ref, g_ref, wm_ref, bias_ref, y_ref):
    cw = y_ref.shape[1]
    t_len = y_ref.shape[0]
    u = _gelu(uv_ref[:, 0:cw])
    v = _rms(_gelu(uv_ref[:, cw:2 * cw]), g_ref[...]).astype(BF16)
    lane = lax.broadcasted_iota(jnp.int32, (GMLP_CHUNK, LANES), 1)
    first_head = lane < GMLP_HEAD_DIM
    for c in range(t_len // GMLP_CHUNK):
        r0 = c * GMLP_CHUNK
        tiles = []
        for j in range(cw // LANES):
            vc = v[r0:r0 + GMLP_CHUNK, j * LANES:(j + 1) * LANES]
            tiles.append(jnp.where(first_head, _dot(wm_ref[2 * j], vc), _dot(wm_ref[2 * j + 1], vc)))
        y = jnp.concatenate(tiles, axis=1) + bias_ref[...]
        y_ref[r0:r0 + GMLP_CHUNK, :] = (u[r0:r0 + GMLP_CHUNK, :] * y).astype(y_ref.dtype)


def _gmlp(uv, vnorm_g, ws, bs):
    n = uv.shape[0]
    cw = uv.shape[1] // 2
    heads = ws.shape[0]
    t = min(T_GMLP, n)
    tril = jnp.tril(jnp.ones((GMLP_CHUNK, GMLP_CHUNK), dtype=bool))
    wm = jnp.where(tril, ws, jnp.zeros_like(ws)).astype(BF16)
    bias = jnp.repeat(bs.T.astype(F32), cw // heads, axis=1)
    return pl.pallas_call(
        _gmlp_kernel,
        grid=(n // t,),
        in_specs=[pl.BlockSpec((t, 2 * cw), lambda i: (i, 0)),
                  pl.BlockSpec((1, cw), lambda i: (0, 0)),
                  pl.BlockSpec(wm.shape, lambda i: (0, 0, 0)),
                  pl.BlockSpec(bias.shape, lambda i: (0, 0))],
        out_specs=pl.BlockSpec((t, cw), lambda i: (i, 0)),
        out_shape=jax.ShapeDtypeStruct((n, cw), BF16),
        compiler_params=_cparams(("parallel",)),
        name="gmlp",
    )(uv, vnorm_g.reshape(1, cw).astype(F32), wm, bias)


def _sb_attn_kernel(q_ref, k_ref, v_ref, tri_ref, o_ref, acc_ref, run_ref):
    qi = pl.program_id(2)
    n_sub = q_ref.shape[0] // TK_SB
    q = q_ref[...]
    lane = lax.broadcasted_iota(jnp.int32, q.shape, 1)
    lane_sub = lax.broadcasted_iota(jnp.int32, (TK_SB, LANES), 1)
    q_h0 = jnp.where(lane < SB_HEAD_DIM, q, jnp.zeros_like(q))
    q_h1 = jnp.where(lane < SB_HEAD_DIM, jnp.zeros_like(q), q)
    row = lax.broadcasted_iota(jnp.int32, (2 * TK_SB, TK_SB), 0)
    col = lax.broadcasted_iota(jnp.int32, (2 * TK_SB, TK_SB), 1)
    below_diag = col < jnp.where(row >= TK_SB, row - TK_SB, row)

    def visit(qs, kb, run, acc, diag):
        okf = jnp.where(kb >= 0, 1.0, 0.0).astype(F32)
        k0 = pl.multiple_of(jnp.maximum(kb, 0) * TK_SB, TK_SB)
        kblk = k_ref[pl.ds(k0, TK_SB), :]
        vblk = v_ref[pl.ds(k0, TK_SB), :]
        z = _dot_nt(qs, kblk)
        sp = jnp.maximum(z, 0.0) + jnp.log(1.0 + jnp.exp(-jnp.abs(z)))
        lm = jnp.where(below_diag, -sp, 0.0) if diag else -sp * okf
        lm_hi = lm.astype(BF16)
        lm_lo = (lm - lm_hi.astype(F32)).astype(BF16)
        cs = _dot(jnp.concatenate([lm_hi, lm_lo], axis=0), tri_ref[...])
        cs = cs[:2 * TK_SB] + cs[2 * TK_SB:]
        w = jnp.exp(z - sp + cs[:, :TK_SB] + run)
        w = jnp.where(below_diag, w, 0.0) if diag else w * okf
        return run + cs[:, TK_SB:], acc + _dot(w.astype(BF16), vblk)

    def stacked_q(sub):
        rows = slice(sub * TK_SB, (sub + 1) * TK_SB)
        return jnp.concatenate([q_h0[rows], q_h1[rows]], axis=0)

    live = False
    for sub in range(n_sub):
        qs = stacked_q(sub)
        run = jnp.zeros((2 * TK_SB, TK_SB), F32)
        acc = jnp.zeros((2 * TK_SB, LANES), F32)
        for o in range(SB_STATIC_BLOCKS):
            run, acc = visit(qs, qi * n_sub + sub - o, run, acc, o == 0)
        run_ref[sub] = run
        acc_ref[sub] = acc
        live = jnp.logical_or(live, jnp.max(run) > SB_LOG_CUTOFF)

    def cond(c):
        o, live = c
        return jnp.logical_and(live, o <= qi * n_sub + n_sub - 1)

    def body(c):
        o, _ = c
        live = False
        for sub in range(n_sub):
            run, acc = visit(stacked_q(sub), qi * n_sub + sub - o, run_ref[sub], acc_ref[sub], False)
            run_ref[sub] = run
            acc_ref[sub] = acc
            live = jnp.logical_or(live, jnp.max(run) > SB_LOG_CUTOFF)
        return o + 1, live

    lax.while_loop(cond, body, (SB_STATIC_BLOCKS, live))
    for sub in range(n_sub):
        acc = acc_ref[sub]
        rows = slice(sub * TK_SB, (sub + 1) * TK_SB)
        o_ref[rows, :] = jnp.where(lane_sub < SB_HEAD_DIM, acc[:TK_SB], acc[TK_SB:]).astype(o_ref.dtype)


def _sb_attention(q, k, v, batch):
    n, dw = q.shape
    seq = n // batch
    tq = min(TQ_SB, seq)
    nq = seq // tq
    npair = dw // LANES
    s_idx = jnp.arange(TK_SB)[:, None]
    j_idx = jnp.arange(TK_SB)[None, :]
    tri = jnp.concatenate([(s_idx > j_idx), jnp.ones((TK_SB, TK_SB), bool)], axis=1).astype(BF16)
    return pl.pallas_call(
        _sb_attn_kernel,
        grid=(batch, npair, nq),
        in_specs=[pl.BlockSpec((tq, LANES), lambda b, h, i: (b * nq + i, h)),
                  pl.BlockSpec((seq, LANES), lambda b, h, i: (b, h)),
                  pl.BlockSpec((seq, LANES), lambda b, h, i: (b, h)),
                  pl.BlockSpec(tri.shape, lambda b, h, i: (0, 0))],
        out_specs=pl.BlockSpec((tq, LANES), lambda b, h, i: (b * nq + i, h)),
        out_shape=jax.ShapeDtypeStruct((n, dw), BF16),
        scratch_shapes=[pltpu.VMEM((tq // TK_SB, 2 * TK_SB, LANES), F32),
                        pltpu.VMEM((tq // TK_SB, 2 * TK_SB, TK_SB), F32)],
        compiler_params=_cparams(("parallel", "parallel", "arbitrary")),
        name="sb_attn",
    )(q, k, v, tri)


def _sort_pairs(n):
    pairs = []
    p = 1
    while p < n:
        k = p
        while k >= 1:
            for j in range(k % p, n - k, 2 * k):
                for i in range(min(k, n - j - k)):
                    if (i + j) // (2 * p) == (i + j + k) // (2 * p):
                        pairs.append((i + j, i + j + k))
            k //= 2
        p *= 2
    return pairs


def _sublane_all(x, op):
    for shift in (4, 2, 1):
        x = op(x, pltpu.roll(x, shift, 0))
    return x


def _top16(tiles):
    t = list(tiles)
    for i, j in _sort_pairs(len(t)):
        t[i], t[j] = jnp.maximum(t[i], t[j]), jnp.minimum(t[i], t[j])
    n = len(t)
    for shift in (4, 2, 1):
        other = [pltpu.roll(x, shift, 0) for x in t]
        t = [jnp.maximum(t[i], other[n - 1 - i]) for i in range(n)]
        d = n // 2
        while d >= 1:
            for i in range(n):
                if not i & d:
                    t[i], t[i + d] = jnp.maximum(t[i], t[i + d]), jnp.minimum(t[i], t[i + d])
            d //= 2
    return t


def _next_below(tiles, bound):
    m = None
    for x in tiles:
        y = jnp.where(x < bound, x, NEG_INF)
        m = y if m is None else jnp.maximum(m, y)
    return _sublane_all(m, jnp.maximum)


def _rows(tiles):
    sub = lax.broadcasted_iota(jnp.int32, tiles[0].shape, 0)
    out = tiles[-1]
    for r in range(len(tiles) - 2, -1, -1):
        out = jnp.where(sub == r, tiles[r], out)
    return out


def _peer_select_kernel(x_ref, ya_ref, yb_ref, wo_ref, g_ref, wq_ref, k1_ref, k2_ref,
                        h_ref, hx_ref, th_ref, e1_ref, s2_ref, e2_ref, s1_ref, s2s_ref):
    t_len = x_ref.shape[0]
    wa = ya_ref.shape[1]
    keys = s1_ref.shape[0]

    @pl.when(pl.program_id(1) == 0)
    def _():
        mix = _dot(ya_ref[...], wo_ref[0:wa, :]) + _dot(yb_ref[...], wo_ref[wa:, :])
        h = x_ref[...] + mix
        h_ref[...] = h
        hx_ref[...] = _rms(h, g_ref[...]).astype(BF16)

    q = _dot(hx_ref[...], wq_ref[...])
    half = q.shape[1] // 2
    s1_ref[...] = _dot_nt(k1_ref[0], q[:, :half].astype(BF16))
    s2s_ref[...] = _dot_nt(k2_ref[0], q[:, half:].astype(BF16))

    i8 = lax.broadcasted_iota(jnp.int32, (8, LANES), 0)
    neg = jnp.full((8, LANES), NEG_INF, F32)

    def chunk(c, carry):
        l0 = pl.multiple_of(c * LANES, LANES)
        s1 = [s1_ref[r:r + 8, pl.ds(l0, LANES)] for r in range(0, keys, 8)]
        s2 = [s2s_ref[r:r + 8, pl.ds(l0, LANES)] for r in range(0, keys, 8)]
        v1 = _top16(s1)
        v2 = _top16(s2)
        v1_17 = _next_below(s1, v1[15])
        v2_17 = _next_below(s2, v2[15])
        v2_lo = _rows(v2[0:8])
        v2_hi = _rows(v2[8:16])
        cands = [v1[0] + v2_lo, v1[0] + v2_hi, v1[1] + v2_lo]
        for a, nb in ((2, 5), (3, 4), (4, 3), (5, 2), (6, 2), (7, 2)):
            cands.append(jnp.where(i8 < nb, v1[a] + v2_lo, NEG_INF))
        cands.append(_rows(v1[8:16]) + v2[0])
        cands.append(jnp.where(i8 == 0, v1[0] + v2_17, jnp.where(i8 == 1, v1_17 + v2[0], NEG_INF)))
        top = _top16(cands + [neg] * (16 - len(cands)))
        t16 = top[15]
        t17 = _next_below(cands, t16)
        tau = jnp.where(t17 == NEG_INF, t16, 0.5 * (t16 + t17))
        m1 = v1[0]
        m2 = v2[0]
        zs = None
        for x in cands:
            y = jnp.where(x >= tau, jnp.exp(x - (m1 + m2)), 0.0)
            zs = y if zs is None else zs + y
        scale = 0.5 / _sublane_all(zs, jnp.add)
        for i in range(keys // 8):
            rows = slice(8 * i, 8 * i + 8)
            th_ref[0, c, rows, :] = tau - s1[i]
            e1_ref[0, c, rows, :] = jnp.exp(s1[i] - m1)
            s2_ref[0, c, rows, :] = s2[i]
            e2_ref[0, c, rows, :] = jnp.exp(s2[i] - m2) * scale
        return carry

    lax.fori_loop(0, t_len // LANES, chunk, 0, unroll=2)


def _peer_select(x, ya, yb, w_out, g, wq, k1, k2):
    n, d = x.shape
    wa = ya.shape[1]
    heads, keys, half = k1.shape
    t = min(T_PEER, n)
    sel_shape = jax.ShapeDtypeStruct((heads, n // LANES, keys, LANES), F32)
    sel_spec = pl.BlockSpec((1, t // LANES, keys, LANES), lambda i, h: (h, i, 0, 0))
    return pl.pallas_call(
        _peer_select_kernel,
        grid=(n // t, heads),
        in_specs=[pl.BlockSpec((t, d), lambda i, h: (i, 0)),
                  pl.BlockSpec((t, wa), lambda i, h: (i, 0)),
                  pl.BlockSpec((t, yb.shape[1]), lambda i, h: (i, 0)),
                  pl.BlockSpec(w_out.shape, lambda i, h: (0, 0)),
                  pl.BlockSpec((1, d), lambda i, h: (0, 0)),
                  pl.BlockSpec((d, 2 * half), lambda i, h: (0, h)),
                  pl.BlockSpec((1, keys, half), lambda i, h: (h, 0, 0)),
                  pl.BlockSpec((1, keys, half), lambda i, h: (h, 0, 0))],
        out_specs=[pl.BlockSpec((t, d), lambda i, h: (i, 0)),
                   pl.BlockSpec((t, d), lambda i, h: (i, 0)),
                   sel_spec, sel_spec, sel_spec, sel_spec],
        out_shape=[jax.ShapeDtypeStruct((n, d), F32), jax.ShapeDtypeStruct((n, d), BF16),
                   sel_shape, sel_shape, sel_shape, sel_shape],
        scratch_shapes=[pltpu.VMEM((keys, t), F32), pltpu.VMEM((keys, t), F32)],
        compiler_params=_cparams(("parallel", "arbitrary")),
        name="peer_select",
    )(x, ya, yb, w_out.astype(BF16), g.reshape(1, d).astype(F32), wq.astype(BF16),
      k1.astype(BF16), k2.astype(BF16))


def _peer_dense_kernel(hx_ref, th_ref, e1_ref, s2_ref, e2_ref, u_ref, vt_ref, res_ref, gf_ref,
                       out_ref, acc_ref, act0_ref, act1_ref, w0_ref, w1_ref, *, final_norm):
    j = pl.program_id(1)
    n_steps = pl.num_programs(1)
    heads, n_lt, keys, _ = th_ref.shape
    t_len = n_lt * LANES
    e_sub = act0_ref.shape[0]
    n_i1 = e_sub // keys
    d_rows = acc_ref.shape[0] // n_i1
    m_rows = 2 * e_sub // n_i1
    tw = t_len // 2
    rc = 32
    c0 = math.sqrt(2.0 / math.pi)
    c1 = c0 * 0.044715

    never = lax.broadcasted_iota(jnp.int32, (rc, LANES), 0) < jnp.minimum(j, 0)

    def slot_loop(sub, do1, do2, do3):
        act_w, act_r = (act1_ref, act0_ref) if sub else (act0_ref, act1_ref)
        w_w, w_r = (w0_ref, w1_ref) if sub else (w1_ref, w0_ref)
        i1_base = (2 * j + sub - 1) * n_i1

        def body(k, carry):
            deps = []
            if do3:
                d0 = pl.multiple_of(k * d_rows, d_rows)
                p3 = _dot(vt_ref[pl.ds(d0, d_rows), sub * e_sub:(sub + 1) * e_sub], w_r[...])
                acc_ref[pl.ds(d0, d_rows), :] += p3
                deps.append([p3[c * (d_rows // n_lt):c * (d_rows // n_lt) + rc, 0:LANES] for c in range(n_lt)])
            if do1:
                m0 = pl.multiple_of((k // 2) * m_rows, m_rows)
                t0 = pl.multiple_of((k % 2) * tw, tw)
                p1 = _dot_nt(u_ref[pl.ds(sub * e_sub + m0, m_rows), :], hx_ref[pl.ds(t0, tw), :])
                act_w[pl.ds(m0, m_rows), pl.ds(t0, tw)] = p1
                deps.append([p1[c * (m_rows // n_lt):c * (m_rows // n_lt) + rc, 0:LANES] for c in range(n_lt)])
            if do2:
                i1 = i1_base + k
                r0 = pl.multiple_of(k * keys, keys)
                for lt in range(n_lt):
                    ls = slice(lt * LANES, (lt + 1) * LANES)
                    th = [jnp.broadcast_to(th_ref[h, lt, pl.ds(i1, 1), :], (rc, LANES)) for h in range(heads)]
                    e1 = [jnp.broadcast_to(e1_ref[h, lt, pl.ds(i1, 1), :], (rc, LANES)) for h in range(heads)]
                    for r in range(0, keys, rc):
                        g = jnp.zeros((rc, LANES), F32)
                        for h in range(heads):
                            g = g + jnp.where(s2_ref[h, lt, r:r + rc, :] >= th[h], e2_ref[h, lt, r:r + rc, :], 0.0) * e1[h]
                        a = act_r[pl.ds(r0 + r, rc), ls]
                        tanh = jnp.tanh(a * (c0 + c1 * (a * a)))
                        ga = g * a
                        if r == keys - rc:
                            for dep in deps:
                                ga = jnp.where(never, dep[lt], ga)
                        w_w[pl.ds(r0 + r, rc), ls] = (ga + ga * tanh).astype(BF16)
            return carry

        lax.fori_loop(0, n_i1, body, 0, unroll=2)

    @pl.when(j == 0)
    def _():
        acc_ref[...] = jnp.zeros_like(acc_ref)
        slot_loop(0, True, False, False)
        slot_loop(1, True, True, False)

    @pl.when(jnp.logical_and(j > 0, j < n_steps - 1))
    def _():
        slot_loop(0, True, True, True)
        slot_loop(1, True, True, True)

    @pl.when(j == n_steps - 1)
    def _():
        slot_loop(0, False, True, True)
        slot_loop(1, False, False, True)
        o = res_ref[...] + acc_ref[...].T
        if final_norm:
            o = _rms(o, gf_ref[...])
        out_ref[...] = o


def _peer_dense(hx, th, e1, s2, e2, u_tab, vt_tab, res, g_final, final_norm):
    n, d = res.shape
    heads, _, keys, _ = th.shape
    n_exp = u_tab.shape[0]
    t = min(T_PEER, n)
    e_blk = min(E_BLK, n_exp)
    nj = n_exp // e_blk
    once = pl.Buffered(1)
    sel_spec = pl.BlockSpec((heads, t // LANES, keys, LANES), lambda i, j: (0, i, 0, 0), pipeline_mode=once)
    return pl.pallas_call(
        functools.partial(_peer_dense_kernel, final_norm=final_norm),
        grid=(n // t, nj + 1),
        in_specs=[pl.BlockSpec((t, d), lambda i, j: (i, 0), pipeline_mode=once),
                  sel_spec, sel_spec, sel_spec, sel_spec,
                  pl.BlockSpec((e_blk, d), lambda i, j: (jnp.minimum(j, nj - 1), 0)),
                  pl.BlockSpec((d, e_blk), lambda i, j: (0, jnp.maximum(j - 1, 0))),
                  pl.BlockSpec((t, d), lambda i, j: (i, 0), pipeline_mode=once),
                  pl.BlockSpec((1, d), lambda i, j: (0, 0))],
        out_specs=pl.BlockSpec((t, d), lambda i, j: (i, 0)),
        out_shape=jax.ShapeDtypeStruct((n, d), F32),
        scratch_shapes=[pltpu.VMEM((d, t), F32),
                        pltpu.VMEM((e_blk // 2, t), F32), pltpu.VMEM((e_blk // 2, t), F32),
                        pltpu.VMEM((e_blk // 2, t), BF16), pltpu.VMEM((e_blk // 2, t), BF16)],
        compiler_params=_cparams(("parallel", "arbitrary")),
        name="peer_dense",
    )(hx, th, e1, s2, e2, u_tab, vt_tab, res, g_final.reshape(1, d).astype(F32))


def _peer_block(x, ya, yb, w_out, g_ffn, wq, k1, k2, u_tab, v_tab, g_final, final_norm):
    h, hx, th, e1, s2, e2 = _peer_select(x, ya, yb, w_out, g_ffn, wq, k1, k2)
    return _peer_dense(hx, th, e1, s2, e2, u_tab.astype(BF16), v_tab.astype(BF16).T, h, g_final,
                       final_norm)


def kernel(x, l0_norm_mix_g, l0_w_in, a_lam_re, a_lam_im, a_log_dt, a_b_re, a_b_im, a_c_re, a_c_im, a_d, a_glu_w, a_glu_b, b_conv_w, l0_w_out, l0_norm_ffn_g, l0_peer_wq, l0_peer_k1, l0_peer_k2, l0_peer_u, l0_peer_v, l1_norm_mix_g, l1_w_in, c_vnorm_g, c_ws, c_bs, l1_w_out, l1_norm_ffn_g, l1_peer_wq, l1_peer_k1, l1_peer_k2, l1_peer_u, l1_peer_v, final_norm_g):
    batch, seq, d = x.shape
    n = batch * seq
    h = x.reshape(n, d)

    (p0,) = _norm_matmul(h, l0_norm_mix_g.astype(F32), l0_w_in.astype(BF16),
                         ((0, l0_w_in.shape[1], 1.0),), (F32,))
    bq, are, aim, cq = _s5_params(a_lam_re, a_lam_im, a_log_dt, a_b_re, a_b_im, a_c_re, a_c_im)
    ya, yb = _even_mixer(p0, batch, bq, are, aim, cq, a_d, a_glu_w, a_glu_b, b_conv_w)
    h = _peer_block(h, ya, yb, l0_w_out, l0_norm_ffn_g, l0_peer_wq, l0_peer_k1, l0_peer_k2,
                    l0_peer_u, l0_peer_v, final_norm_g, False)

    cw = c_vnorm_g.shape[0]
    dw = (l1_w_in.shape[1] - 2 * cw) // 3
    o = 2 * cw
    splits = ((0, o, 1.0), (o, o + dw, SB_HEAD_DIM ** -0.5), (o + dw, o + 2 * dw, 1.0),
              (o + 2 * dw, o + 3 * dw, 1.0))
    uv, q, k, v = _norm_matmul(h, l1_norm_mix_g.astype(F32), l1_w_in.astype(BF16), splits,
                               (F32, BF16, BF16, BF16))
    yc = _gmlp(uv, c_vnorm_g, c_ws, c_bs)
    yd = _sb_attention(q, k, v, batch)
    h = _peer_block(h, yc, yd, l1_w_out, l1_norm_ffn_g, l1_peer_wq, l1_peer_k1, l1_peer_k2,
                    l1_peer_u, l1_peer_v, final_norm_g, True)
    return h.reshape(batch, seq, d)
```

```python
import functools
import math

import jax
import jax.numpy as jnp
from jax import lax
from jax.experimental import pallas as pl
from jax.experimental.pallas import tpu as pltpu

F32 = jnp.float32
BF16 = jnp.bfloat16
NEG_INF = float("-inf")

RMS_EPS = 1e-6
LANES = 128
S5_GROUP = 16
S5_STATE = 64
S5_QUARTERS = 4
PEER_HEADS = 8
PEER_KEYS = 128
PEER_TOPK = 16
SB_HEAD_DIM = 64
GMLP_CHUNK = 128
GMLP_HEAD_DIM = 64
SB_LOG_CUTOFF = -104.0

TM_PROJ = 512
T_S5 = 256
T_GMLP = 512
TQ_SB = 256
TK_SB = 128
SB_STATIC_BLOCKS = 3
T_PEER = 1024
E_BLK = 1024
VMEM_LIMIT = 58 * 1024 * 1024


def _cparams(sem, flags=None):
    return pltpu.CompilerParams(dimension_semantics=sem, vmem_limit_bytes=VMEM_LIMIT, flags=flags)


def _dot(a, b):
    return jnp.dot(a, b, preferred_element_type=F32)


def _dot_nt(a, b):
    return lax.dot_general(a, b, (((1,), (1,)), ((), ())), preferred_element_type=F32)


def _gelu(x):
    c = math.sqrt(2.0 / math.pi)
    return 0.5 * x * (1.0 + jnp.tanh(c * (x + 0.044715 * (x * x * x))))


def _rms(x, g):
    return x * lax.rsqrt(jnp.mean(x * x, axis=-1, keepdims=True) + RMS_EPS) * g


def _norm_matmul_kernel(x_ref, g_ref, w_ref, *out_refs, splits):
    xn = _rms(x_ref[...], g_ref[...]).astype(BF16)
    p = _dot(xn, w_ref[...])
    for o_ref, (lo, hi, scale) in zip(out_refs, splits):
        part = p[:, lo:hi]
        if scale != 1.0:
            part = part * scale
        o_ref[...] = part.astype(o_ref.dtype)


def _norm_matmul(x, g, w, splits, dtypes):
    n, d = x.shape
    m = w.shape[1]
    tm = min(TM_PROJ, n)
    out_shape = [jax.ShapeDtypeStruct((n, hi - lo), dt) for (lo, hi, _), dt in zip(splits, dtypes)]
    out_specs = [pl.BlockSpec((tm, hi - lo), lambda i: (i, 0)) for (lo, hi, _) in splits]
    return pl.pallas_call(
        functools.partial(_norm_matmul_kernel, splits=splits),
        grid=(n // tm,),
        in_specs=[pl.BlockSpec((tm, d), lambda i: (i, 0)),
                  pl.BlockSpec((1, d), lambda i: (0, 0)),
                  pl.BlockSpec((d, m), lambda i: (0, 0))],
        out_specs=out_specs,
        out_shape=out_shape,
        compiler_params=_cparams(("parallel",)),
        name="norm_matmul",
    )(x, g.reshape(1, d), w)


def _even_mixer_kernel(p_ref, bq_ref, are_ref, aim_ref, cq_ref, d_ref, gw_ref, gb_ref, cw_ref,
                       ya_ref, yb_ref, bu_ref, hst_ref, zs_ref):
    t_len = p_ref.shape[0]
    aw = ya_ref.shape[1]
    qs = 2 * aw
    half = qs // 2

    @pl.when(pl.program_id(1) == 0)
    def _():
        hst_ref[...] = jnp.zeros_like(hst_ref)
        zs_ref[0:8, :] = jnp.zeros((8, zs_ref.shape[1]), F32)

    u = p_ref[:, 0:aw]
    ub = u.astype(BF16)
    for q in range(S5_QUARTERS):
        bu_ref[:, q * qs:(q + 1) * qs] = _dot(ub[:, q * LANES:(q + 1) * LANES], bq_ref[q])

    for q in range(S5_QUARTERS):
        ar = are_ref[q:q + 1, :]
        ai = aim_ref[q:q + 1, :]
        re_lo = q * qs
        im_lo = q * qs + half

        def step(t, carry, ar=ar, ai=ai, re_lo=re_lo, im_lo=im_lo):
            hr, hi = carry
            br = bu_ref[pl.ds(t, 1), re_lo:re_lo + half]
            bi = bu_ref[pl.ds(t, 1), im_lo:im_lo + half]
            nr = ar * hr - ai * hi + br
            ni = ar * hi + ai * hr + bi
            bu_ref[pl.ds(t, 1), re_lo:re_lo + half] = nr
            bu_ref[pl.ds(t, 1), im_lo:im_lo + half] = ni
            return nr, ni

        hr, hi = lax.fori_loop(0, t_len, step,
                               (hst_ref[2 * q:2 * q + 1, :], hst_ref[2 * q + 1:2 * q + 2, :]), unroll=8)
        hst_ref[2 * q:2 * q + 1, :] = hr
        hst_ref[2 * q + 1:2 * q + 2, :] = hi

    ys = [_dot(bu_ref[:, q * qs:(q + 1) * qs].astype(BF16), cq_ref[q]) for q in range(S5_QUARTERS)]
    y = jnp.concatenate(ys, axis=1) + d_ref[...] * u
    y = _gelu(y)
    gate = jax.nn.sigmoid(_dot(y.astype(BF16), gw_ref[...]) + gb_ref[...])
    ya_ref[...] = (y * gate).astype(ya_ref.dtype)

    bw = yb_ref.shape[1]
    z = p_ref[:, aw + bw:aw + 2 * bw] * p_ref[:, aw + 2 * bw:aw + 3 * bw]
    zs_ref[8:8 + t_len, :] = z
    conv = (cw_ref[2:3, :] * z + cw_ref[1:2, :] * zs_ref[7:7 + t_len, :]
            + cw_ref[0:1, :] * zs_ref[6:6 + t_len, :])
    yb_ref[...] = (p_ref[:, aw:aw + bw] * conv).astype(yb_ref.dtype)
    zs_ref[0:8, :] = zs_ref[t_len:t_len + 8, :]


def _s5_params(lam_re, lam_im, log_dt, b_re, b_im, c_re, c_im):
    g, p = lam_re.shape
    gq = g // S5_QUARTERS
    dt = jnp.exp(log_dt.astype(F32))[:, None]
    lr = lam_re.astype(F32)
    li = lam_im.astype(F32)
    mag = jnp.exp(lr * dt)
    ar = mag * jnp.cos(li * dt)
    ai = mag * jnp.sin(li * dt)
    den = lr * lr + li * li
    nr = ar - 1.0
    fr = (nr * lr + ai * li) / den
    fi = (ai * lr - nr * li) / den
    bre = b_re.astype(F32)
    bim = b_im.astype(F32)
    bbar_re = fr[..., None] * bre - fi[..., None] * bim
    bbar_im = fr[..., None] * bim + fi[..., None] * bre
    eye = jnp.eye(gq, dtype=F32)

    def in_blockdiag(m):
        m = m.reshape(S5_QUARTERS, gq, p, S5_GROUP)
        return jnp.einsum('qgpc,gh->qgchp', m, eye).reshape(S5_QUARTERS, gq * S5_GROUP, gq * p)

    def out_blockdiag(m):
        m = m.reshape(S5_QUARTERS, gq, S5_GROUP, p)
        return jnp.einsum('qgcp,gh->qgphc', m, eye).reshape(S5_QUARTERS, gq * p, gq * S5_GROUP)

    bq = jnp.concatenate([in_blockdiag(bbar_re), in_blockdiag(bbar_im)], axis=2).astype(BF16)
    cq = jnp.concatenate([out_blockdiag(c_re.astype(F32)), -out_blockdiag(c_im.astype(F32))],
                         axis=1).astype(BF16)
    are = ar.reshape(S5_QUARTERS, gq * p)
    aim = ai.reshape(S5_QUARTERS, gq * p)
    return bq, are, aim, cq


def _even_mixer(p, batch, bq, are, aim, cq, d_skip, glu_w, glu_b, conv_w):
    n, width = p.shape
    seq = n // batch
    aw = d_skip.size
    bw = conv_w.shape[1]
    t = min(T_S5, seq)
    nt = seq // t
    qs = 2 * aw
    full = lambda shape: pl.BlockSpec(shape, lambda b, i: (0,) * len(shape))
    return pl.pallas_call(
        _even_mixer_kernel,
        grid=(batch, nt),
        in_specs=[pl.BlockSpec((t, width), lambda b, i: (b * nt + i, 0)),
                  full(bq.shape), full(are.shape), full(aim.shape), full(cq.shape),
                  full((1, aw)), full(glu_w.shape), full((1, aw)), full(conv_w.shape)],
        out_specs=[pl.BlockSpec((t, aw), lambda b, i: (b * nt + i, 0)),
                   pl.BlockSpec((t, bw), lambda b, i: (b * nt + i, 0))],
        out_shape=[jax.ShapeDtypeStruct((n, aw), BF16), jax.ShapeDtypeStruct((n, bw), BF16)],
        scratch_shapes=[pltpu.VMEM((t, S5_QUARTERS * qs), F32),
                        pltpu.VMEM((2 * S5_QUARTERS, aw), F32),
                        pltpu.VMEM((t + 8, bw), F32)],
        compiler_params=_cparams(("arbitrary", "arbitrary")),
        name="even_mixer",
    )(p, bq, are, aim, cq, d_skip.reshape(1, aw).astype(F32), glu_w.astype(BF16),
      glu_b.reshape(1, aw).astype(F32), conv_w.astype(F32))


def _gmlp_kernel(uv_ref, g_ref, wm_ref, bias_ref, y_ref):
    cw = y_ref.shape[1]
    t_len = y_ref.shape[0]
    u = _gelu(uv_ref[:, 0:cw])
    v = _rms(_gelu(uv_ref[:, cw:2 * cw]), g_ref[...]).astype(BF16)
    lane = lax.broadcasted_iota(jnp.int32, (GMLP_CHUNK, LANES), 1)
    first_head = lane < GMLP_HEAD_DIM
    for c in range(t_len // GMLP_CHUNK):
        r0 = c * GMLP_CHUNK
        tiles = []
        for j in range(cw // LANES):
            vc = v[r0:r0 + GMLP_CHUNK, j * LANES:(j + 1) * LANES]
            tiles.append(jnp.where(first_head, _dot(wm_ref[2 * j], vc), _dot(wm_ref[2 * j + 1], vc)))
        y = jnp.concatenate(tiles, axis=1) + bias_ref[...]
        y_ref[r0:r0 + GMLP_CHUNK, :] = (u[r0:r0 + GMLP_CHUNK, :] * y).astype(y_ref.dtype)


def _gmlp(uv, vnorm_g, ws, bs):
    n = uv.shape[0]
    cw = uv.shape[1] // 2
    heads = ws.shape[0]
    t = min(T_GMLP, n)
    tril = jnp.tril(jnp.ones((GMLP_CHUNK, GMLP_CHUNK), dtype=bool))
    wm = jnp.where(tril, ws, jnp.zeros_like(ws)).astype(BF16)
    bias = jnp.repeat(bs.T.astype(F32), cw // heads, axis=1)
    return pl.pallas_call(
        _gmlp_kernel,
        grid=(n // t,),
        in_specs=[pl.BlockSpec((t, 2 * cw), lambda i: (i, 0)),
                  pl.BlockSpec((1, cw), lambda i: (0, 0)),
                  pl.BlockSpec(wm.shape, lambda i: (0, 0, 0)),
                  pl.BlockSpec(bias.shape, lambda i: (0, 0))],
        out_specs=pl.BlockSpec((t, cw), lambda i: (i, 0)),
        out_shape=jax.ShapeDtypeStruct((n, cw), BF16),
        compiler_params=_cparams(("parallel",)),
        name="gmlp",
    )(uv, vnorm_g.reshape(1, cw).astype(F32), wm, bias)


def _sb_attn_kernel(q_ref, k_ref, v_ref, tri_ref, o_ref, acc_ref, run_ref):
    qi = pl.program_id(2)
    n_sub = q_ref.shape[0] // TK_SB
    q = q_ref[...]
    lane = lax.broadcasted_iota(jnp.int32, q.shape, 1)
    lane_sub = lax.broadcasted_iota(jnp.int32, (TK_SB, LANES), 1)
    q_h0 = jnp.where(lane < SB_HEAD_DIM, q, jnp.zeros_like(q))
    q_h1 = jnp.where(lane < SB_HEAD_DIM, jnp.zeros_like(q), q)
    row = lax.broadcasted_iota(jnp.int32, (2 * TK_SB, TK_SB), 0)
    col = lax.broadcasted_iota(jnp.int32, (2 * TK_SB, TK_SB), 1)
    below_diag = col < jnp.where(row >= TK_SB, row - TK_SB, row)

    def visit(qs, kb, run, acc, diag):
        okf = jnp.where(kb >= 0, 1.0, 0.0).astype(F32)
        k0 = pl.multiple_of(jnp.maximum(kb, 0) * TK_SB, TK_SB)
        kblk = k_ref[pl.ds(k0, TK_SB), :]
        vblk = v_ref[pl.ds(k0, TK_SB), :]
        z = _dot_nt(qs, kblk)
        sp = jnp.maximum(z, 0.0) + jnp.log(1.0 + jnp.exp(-jnp.abs(z)))
        lm = jnp.where(below_diag, -sp, 0.0) if diag else -sp * okf
        lm_hi = lm.astype(BF16)
        lm_lo = (lm - lm_hi.astype(F32)).astype(BF16)
        cs = _dot(jnp.concatenate([lm_hi, lm_lo], axis=0), tri_ref[...])
        cs = cs[:2 * TK_SB] + cs[2 * TK_SB:]
        w = jnp.exp(z - sp + cs[:, :TK_SB] + run)
        w = jnp.where(below_diag, w, 0.0) if diag else w * okf
        return run + cs[:, TK_SB:], acc + _dot(w.astype(BF16), vblk)

    def stacked_q(sub):
        rows = slice(sub * TK_SB, (sub + 1) * TK_SB)
        return jnp.concatenate([q_h0[rows], q_h1[rows]], axis=0)

    live = False
    for sub in range(n_sub):
        qs = stacked_q(sub)
        run = jnp.zeros((2 * TK_SB, TK_SB), F32)
        acc = jnp.zeros((2 * TK_SB, LANES), F32)
        for o in range(SB_STATIC_BLOCKS):
            run, acc = visit(qs, qi * n_sub + sub - o, run, acc, o == 0)
        run_ref[sub] = run
        acc_ref[sub] = acc
        live = jnp.logical_or(live, jnp.max(run) > SB_LOG_CUTOFF)

    def cond(c):
        o, live = c
        return jnp.logical_and(live, o <= qi * n_sub + n_sub - 1)

    def body(c):
        o, _ = c
        live = False
        for sub in range(n_sub):
            run, acc = visit(stacked_q(sub), qi * n_sub + sub - o, run_ref[sub], acc_ref[sub], False)
            run_ref[sub] = run
            acc_ref[sub] = acc
            live = jnp.logical_or(live, jnp.max(run) > SB_LOG_CUTOFF)
        return o + 1, live

    lax.while_loop(cond, body, (SB_STATIC_BLOCKS, live))
    for sub in range(n_sub):
        acc = acc_ref[sub]
        rows = slice(sub * TK_SB, (sub + 1) * TK_SB)
        o_ref[rows, :] = jnp.where(lane_sub < SB_HEAD_DIM, acc[:TK_SB], acc[TK_SB:]).astype(o_ref.dtype)


def _sb_attention(q, k, v, batch):
    n, dw = q.shape
    seq = n // batch
    tq = min(TQ_SB, seq)
    nq = seq // tq
    npair = dw // LANES
    s_idx = jnp.arange(TK_SB)[:, None]
    j_idx = jnp.arange(TK_SB)[None, :]
    tri = jnp.concatenate([(s_idx > j_idx), jnp.ones((TK_SB, TK_SB), bool)], axis=1).astype(BF16)
    return pl.pallas_call(
        _sb_attn_kernel,
        grid=(batch, npair, nq),
        in_specs=[pl.BlockSpec((tq, LANES), lambda b, h, i: (b * nq + i, h)),
                  pl.BlockSpec((seq, LANES), lambda b, h, i: (b, h)),
                  pl.BlockSpec((seq, LANES), lambda b, h, i: (b, h)),
                  pl.BlockSpec(tri.shape, lambda b, h, i: (0, 0))],
        out_specs=pl.BlockSpec((tq, LANES), lambda b, h, i: (b * nq + i, h)),
        out_shape=jax.ShapeDtypeStruct((n, dw), BF16),
        scratch_shapes=[pltpu.VMEM((tq // TK_SB, 2 * TK_SB, LANES), F32),
                        pltpu.VMEM((tq // TK_SB, 2 * TK_SB, TK_SB), F32)],
        compiler_params=_cparams(("parallel", "parallel", "arbitrary")),
        name="sb_attn",
    )(q, k, v, tri)


def _sort_pairs(n):
    pairs = []
    p = 1
    while p < n:
        k = p
        while k >= 1:
            for j in range(k % p, n - k, 2 * k):
                for i in range(min(k, n - j - k)):
                    if (i + j) // (2 * p) == (i + j + k) // (2 * p):
                        pairs.append((i + j, i + j + k))
            k //= 2
        p *= 2
    return pairs


def _sublane_all(x, op):
    for shift in (4, 2, 1):
        x = op(x, pltpu.roll(x, shift, 0))
    return x


def _top16(tiles):
    t = list(tiles)
    for i, j in _sort_pairs(len(t)):
        t[i], t[j] = jnp.maximum(t[i], t[j]), jnp.minimum(t[i], t[j])
    n = len(t)
    for shift in (4, 2, 1):
        other = [pltpu.roll(x, shift, 0) for x in t]
        t = [jnp.maximum(t[i], other[n - 1 - i]) for i in range(n)]
        d = n // 2
        while d >= 1:
            for i in range(n):
                if not i & d:
                    t[i], t[i + d] = jnp.maximum(t[i], t[i + d]), jnp.minimum(t[i], t[i + d])
            d //= 2
    return t


def _next_below(tiles, bound):
    m = None
    for x in tiles:
        y = jnp.where(x < bound, x, NEG_INF)
        m = y if m is None else jnp.maximum(m, y)
    return _sublane_all(m, jnp.maximum)


def _rows(tiles):
    sub = lax.broadcasted_iota(jnp.int32, tiles[0].shape, 0)
    out = tiles[-1]
    for r in range(len(tiles) - 2, -1, -1):
        out = jnp.where(sub == r, tiles[r], out)
    return out


def _peer_select_kernel(x_ref, ya_ref, yb_ref, wo_ref, g_ref, wq_ref, k1_ref, k2_ref,
                        h_ref, hx_ref, th_ref, e1_ref, s2_ref, e2_ref, s1_ref, s2s_ref, q_ref):
    t_len = x_ref.shape[0]
    wa = ya_ref.shape[1]
    keys = s1_ref.shape[0]

    @pl.when(pl.program_id(1) == 0)
    def _():
        mix = _dot(ya_ref[...], wo_ref[0:wa, :]) + _dot(yb_ref[...], wo_ref[wa:, :])
        h = x_ref[...] + mix
        h_ref[...] = h
        hx = _rms(h, g_ref[...]).astype(BF16)
        hx_ref[...] = hx
        q_ref[...] = _dot(hx, wq_ref[...]).astype(BF16)

    half = k1_ref.shape[2]
    q0 = pl.multiple_of(pl.program_id(1) * (2 * half), 2 * half)
    s1_ref[...] = _dot_nt(k1_ref[0], q_ref[:, pl.ds(q0, half)])
    s2s_ref[...] = _dot_nt(k2_ref[0], q_ref[:, pl.ds(q0 + half, half)])

    i8 = lax.broadcasted_iota(jnp.int32, (8, LANES), 0)
    neg = jnp.full((8, LANES), NEG_INF, F32)

    def chunk(c, carry):
        l0 = pl.multiple_of(c * LANES, LANES)
        s1 = [s1_ref[r:r + 8, pl.ds(l0, LANES)] for r in range(0, keys, 8)]
        s2 = [s2s_ref[r:r + 8, pl.ds(l0, LANES)] for r in range(0, keys, 8)]
        v1 = _top16(s1)
        v2 = _top16(s2)
        v1_17 = _next_below(s1, v1[15])
        v2_17 = _next_below(s2, v2[15])
        v2_lo = _rows(v2[0:8])
        v2_hi = _rows(v2[8:16])
        cands = [v1[0] + v2_lo, v1[0] + v2_hi, v1[1] + v2_lo]
        for a, nb in ((2, 5), (3, 4), (4, 3), (5, 2), (6, 2), (7, 2)):
            cands.append(jnp.where(i8 < nb, v1[a] + v2_lo, NEG_INF))
        cands.append(_rows(v1[8:16]) + v2[0])
        cands.append(jnp.where(i8 == 0, v1[0] + v2_17, jnp.where(i8 == 1, v1_17 + v2[0], NEG_INF)))
        top = _top16(cands + [neg] * (16 - len(cands)))
        t16 = top[15]
        t17 = _next_below(cands, t16)
        tau = jnp.where(t17 == NEG_INF, t16, 0.5 * (t16 + t17))
        m1 = v1[0]
        m2 = v2[0]
        zs = None
        for x in cands:
            y = jnp.where(x >= tau, jnp.exp(x - (m1 + m2)), 0.0)
            zs = y if zs is None else zs + y
        scale = 0.5 / _sublane_all(zs, jnp.add)
        for i in range(keys // 8):
            rows = slice(8 * i, 8 * i + 8)
            th_ref[0, c, rows, :] = tau - s1[i]
            e1_ref[0, c, rows, :] = jnp.exp(s1[i] - m1)
            s2_ref[0, c, rows, :] = s2[i]
            e2_ref[0, c, rows, :] = jnp.exp(s2[i] - m2) * scale
        return carry

    lax.fori_loop(0, t_len // LANES, chunk, 0, unroll=2)


def _peer_select(x, ya, yb, w_out, g, wq, k1, k2):
    n, d = x.shape
    wa = ya.shape[1]
    heads, keys, half = k1.shape
    t = min(T_PEER, n)
    sel_shape = jax.ShapeDtypeStruct((heads, n // LANES, keys, LANES), F32)
    sel_spec = pl.BlockSpec((1, t // LANES, keys, LANES), lambda i, h: (h, i, 0, 0))
    return pl.pallas_call(
        _peer_select_kernel,
        grid=(n // t, heads),
        in_specs=[pl.BlockSpec((t, d), lambda i, h: (i, 0)),
                  pl.BlockSpec((t, wa), lambda i, h: (i, 0)),
                  pl.BlockSpec((t, yb.shape[1]), lambda i, h: (i, 0)),
                  pl.BlockSpec(w_out.shape, lambda i, h: (0, 0)),
                  pl.BlockSpec((1, d), lambda i, h: (0, 0)),
                  pl.BlockSpec(wq.shape, lambda i, h: (0, 0)),
                  pl.BlockSpec((1, keys, half), lambda i, h: (h, 0, 0)),
                  pl.BlockSpec((1, keys, half), lambda i, h: (h, 0, 0))],
        out_specs=[pl.BlockSpec((t, d), lambda i, h: (i, 0)),
                   pl.BlockSpec((t, d), lambda i, h: (i, 0)),
                   sel_spec, sel_spec, sel_spec, sel_spec],
        out_shape=[jax.ShapeDtypeStruct((n, d), F32), jax.ShapeDtypeStruct((n, d), BF16),
                   sel_shape, sel_shape, sel_shape, sel_shape],
        scratch_shapes=[pltpu.VMEM((keys, t), F32), pltpu.VMEM((keys, t), F32),
                        pltpu.VMEM((t, wq.shape[1]), BF16)],
        compiler_params=_cparams(("parallel", "arbitrary")),
        name="peer_select",
    )(x, ya, yb, w_out.astype(BF16), g.reshape(1, d).astype(F32), wq.astype(BF16),
      k1.astype(BF16), k2.astype(BF16))


def _peer_dense_kernel(hx_ref, th_ref, e1_ref, s2_ref, e2_ref, u_ref, vt_ref, res_ref, gf_ref,
                       out_ref, acc_ref, act0_ref, act1_ref, w0_ref, w1_ref, *, final_norm):
    j = pl.program_id(1)
    n_steps = pl.num_programs(1)
    heads, n_lt, keys, _ = th_ref.shape
    t_len = n_lt * LANES
    e_sub = act0_ref.shape[0]
    n_i1 = e_sub // keys
    d_rows = acc_ref.shape[0] // n_i1
    m_rows = 2 * e_sub // n_i1
    tw = t_len // 2
    rc = 32
    c0 = math.sqrt(2.0 / math.pi)
    c1 = c0 * 0.044715

    never = lax.broadcasted_iota(jnp.int32, (rc, LANES), 0) < jnp.minimum(j, 0)

    def slot_loop(sub, do1, do2, do3):
        act_w, act_r = (act1_ref, act0_ref) if sub else (act0_ref, act1_ref)
        w_w, w_r = (w0_ref, w1_ref) if sub else (w1_ref, w0_ref)
        i1_base = (2 * j + sub - 1) * n_i1

        def body(k, carry):
            deps = []
            if do3:
                d0 = pl.multiple_of(k * d_rows, d_rows)
                p3 = _dot(vt_ref[pl.ds(d0, d_rows), sub * e_sub:(sub + 1) * e_sub], w_r[...])
                acc_ref[pl.ds(d0, d_rows), :] += p3
                deps.append([p3[c * (d_rows // n_lt):c * (d_rows // n_lt) + rc, 0:LANES] for c in range(n_lt)])
            if do1:
                m0 = pl.multiple_of((k // 2) * m_rows, m_rows)
                t0 = pl.multiple_of((k % 2) * tw, tw)
                p1 = _dot_nt(u_ref[pl.ds(sub * e_sub + m0, m_rows), :], hx_ref[pl.ds(t0, tw), :])
                act_w[pl.ds(m0, m_rows), pl.ds(t0, tw)] = p1
                deps.append([p1[c * (m_rows // n_lt):c * (m_rows // n_lt) + rc, 0:LANES] for c in range(n_lt)])
            if do2:
                i1 = i1_base + k
                r0 = pl.multiple_of(k * keys, keys)
                for lt in range(n_lt):
                    ls = slice(lt * LANES, (lt + 1) * LANES)
                    th = [jnp.broadcast_to(th_ref[h, lt, pl.ds(i1, 1), :], (rc, LANES)) for h in range(heads)]
                    e1 = [jnp.broadcast_to(e1_ref[h, lt, pl.ds(i1, 1), :], (rc, LANES)) for h in range(heads)]
                    for r in range(0, keys, rc):
                        g = jnp.zeros((rc, LANES), F32)
                        for h in range(heads):
                            g = g + jnp.where(s2_ref[h, lt, r:r + rc, :] >= th[h], e2_ref[h, lt, r:r + rc, :], 0.0) * e1[h]
                        a = act_r[pl.ds(r0 + r, rc), ls]
                        tanh = jnp.tanh(a * (c0 + c1 * (a * a)))
                        ga = g * a
                        if r == keys - rc:
                            for dep in deps:
                                ga = jnp.where(never, dep[lt], ga)
                        w_w[pl.ds(r0 + r, rc), ls] = (ga + ga * tanh).astype(BF16)
            return carry

        lax.fori_loop(0, n_i1, body, 0, unroll=2)

    @pl.when(j == 0)
    def _():
        acc_ref[...] = jnp.zeros_like(acc_ref)
        slot_loop(0, True, False, False)
        slot_loop(1, True, True, False)

    @pl.when(jnp.logical_and(j > 0, j < n_steps - 1))
    def _():
        slot_loop(0, True, True, True)
        slot_loop(1, True, True, True)

    @pl.when(j == n_steps - 1)
    def _():
        slot_loop(0, False, True, True)
        slot_loop(1, False, False, True)
        o = res_ref[...] + acc_ref[...].T
        if final_norm:
            o = _rms(o, gf_ref[...])
        out_ref[...] = o


def _peer_dense(hx, th, e1, s2, e2, u_tab, vt_tab, res, g_final, final_norm):
    n, d = res.shape
    heads, _, keys, _ = th.shape
    n_exp = u_tab.shape[0]
    t = min(T_PEER, n)
    e_blk = min(E_BLK, n_exp)
    nj = n_exp // e_blk
    once = pl.Buffered(1)
    sel_spec = pl.BlockSpec((heads, t // LANES, keys, LANES), lambda i, j: (0, i, 0, 0), pipeline_mode=once)
    return pl.pallas_call(
        functools.partial(_peer_dense_kernel, final_norm=final_norm),
        grid=(n // t, nj + 1),
        in_specs=[pl.BlockSpec((t, d), lambda i, j: (i, 0), pipeline_mode=once),
                  sel_spec, sel_spec, sel_spec, sel_spec,
                  pl.BlockSpec((e_blk, d), lambda i, j: (jnp.minimum(j, nj - 1), 0)),
                  pl.BlockSpec((d, e_blk), lambda i, j: (0, jnp.maximum(j - 1, 0))),
                  pl.BlockSpec((t, d), lambda i, j: (i, 0), pipeline_mode=once),
                  pl.BlockSpec((1, d), lambda i, j: (0, 0))],
        out_specs=pl.BlockSpec((t, d), lambda i, j: (i, 0)),
        out_shape=jax.ShapeDtypeStruct((n, d), F32),
        scratch_shapes=[pltpu.VMEM((d, t), F32),
                        pltpu.VMEM((e_blk // 2, t), F32), pltpu.VMEM((e_blk // 2, t), F32),
                        pltpu.VMEM((e_blk // 2, t), BF16), pltpu.VMEM((e_blk // 2, t), BF16)],
        compiler_params=_cparams(("parallel", "arbitrary")),
        name="peer_dense",
    )(hx, th, e1, s2, e2, u_tab, vt_tab, res, g_final.reshape(1, d).astype(F32))


def _peer_block(x, ya, yb, w_out, g_ffn, wq, k1, k2, u_tab, v_tab, g_final, final_norm):
    h, hx, th, e1, s2, e2 = _peer_select(x, ya, yb, w_out, g_ffn, wq, k1, k2)
    return _peer_dense(hx, th, e1, s2, e2, u_tab.astype(BF16), v_tab.astype(BF16).T, h, g_final,
                       final_norm)


def kernel(x, l0_norm_mix_g, l0_w_in, a_lam_re, a_lam_im, a_log_dt, a_b_re, a_b_im, a_c_re, a_c_im, a_d, a_glu_w, a_glu_b, b_conv_w, l0_w_out, l0_norm_ffn_g, l0_peer_wq, l0_peer_k1, l0_peer_k2, l0_peer_u, l0_peer_v, l1_norm_mix_g, l1_w_in, c_vnorm_g, c_ws, c_bs, l1_w_out, l1_norm_ffn_g, l1_peer_wq, l1_peer_k1, l1_peer_k2, l1_peer_u, l1_peer_v, final_norm_g):
    batch, seq, d = x.shape
    n = batch * seq
    h = x.reshape(n, d)

    (p0,) = _norm_matmul(h, l0_norm_mix_g.astype(F32), l0_w_in.astype(BF16),
                         ((0, l0_w_in.shape[1], 1.0),), (F32,))
    bq, are, aim, cq = _s5_params(a_lam_re, a_lam_im, a_log_dt, a_b_re, a_b_im, a_c_re, a_c_im)
    ya, yb = _even_mixer(p0, batch, bq, are, aim, cq, a_d, a_glu_w, a_glu_b, b_conv_w)
    h = _peer_block(h, ya, yb, l0_w_out, l0_norm_ffn_g, l0_peer_wq, l0_peer_k1, l0_peer_k2,
                    l0_peer_u, l0_peer_v, final_norm_g, False)

    cw = c_vnorm_g.shape[0]
    dw = (l1_w_in.shape[1] - 2 * cw) // 3
    o = 2 * cw
    splits = ((0, o, 1.0), (o, o + dw, SB_HEAD_DIM ** -0.5), (o + dw, o + 2 * dw, 1.0),
              (o + 2 * dw, o + 3 * dw, 1.0))
    uv, q, k, v = _norm_matmul(h, l1_norm_mix_g.astype(F32), l1_w_in.astype(BF16), splits,
                               (F32, BF16, BF16, BF16))
    yc = _gmlp(uv, c_vnorm_g, c_ws, c_bs)
    yd = _sb_attention(q, k, v, batch)
    h = _peer_block(h, yc, yd, l1_w_out, l1_norm_ffn_g, l1_peer_wq, l1_peer_k1, l1_peer_k2,
                    l1_peer_u, l1_peer_v, final_norm_g, True)
    return h.reshape(batch, seq, d)
```

```python
import functools
import math

import jax
import jax.numpy as jnp
from jax import lax
from jax.experimental import pallas as pl
from jax.experimental.pallas import tpu as pltpu

F32 = jnp.float32
BF16 = jnp.bfloat16
NEG_INF = float("-inf")

RMS_EPS = 1e-6
LANES = 128
S5_GROUP = 16
S5_STATE = 64
S5_QUARTERS = 4
S5_SEGMENTS = 8
S5_TOGETHER = 2
S5_PITCH = 40
PEER_HEADS = 8
PEER_KEYS = 128
PEER_TOPK = 16
SB_HEAD_DIM = 64
GMLP_CHUNK = 128
GMLP_HEAD_DIM = 64
SB_LOG_CUTOFF = -104.0

TM_PROJ = 512
T_S5 = 256
T_GMLP = 512
TQ_SB = 256
TK_SB = 128
SB_STATIC_BLOCKS = 3
T_PEER = 1024
E_BLK = 1024
VMEM_LIMIT = 58 * 1024 * 1024


def _cparams(sem, flags=None):
    return pltpu.CompilerParams(dimension_semantics=sem, vmem_limit_bytes=VMEM_LIMIT, flags=flags)


def _dot(a, b):
    return jnp.dot(a, b, preferred_element_type=F32)


def _dot_nt(a, b):
    return lax.dot_general(a, b, (((1,), (1,)), ((), ())), preferred_element_type=F32)


def _gelu(x):
    c = math.sqrt(2.0 / math.pi)
    return 0.5 * x * (1.0 + jnp.tanh(c * (x + 0.044715 * (x * x * x))))


def _rms(x, g):
    return x * lax.rsqrt(jnp.mean(x * x, axis=-1, keepdims=True) + RMS_EPS) * g


def _norm_matmul_kernel(x_ref, g_ref, w_ref, *out_refs, splits):
    xn = _rms(x_ref[...], g_ref[...]).astype(BF16)
    p = _dot(xn, w_ref[...])
    for o_ref, (lo, hi, scale) in zip(out_refs, splits):
        part = p[:, lo:hi]
        if scale != 1.0:
            part = part * scale
        o_ref[...] = part.astype(o_ref.dtype)


def _norm_matmul(x, g, w, splits, dtypes):
    n, d = x.shape
    m = w.shape[1]
    tm = min(TM_PROJ, n)
    out_shape = [jax.ShapeDtypeStruct((n, hi - lo), dt) for (lo, hi, _), dt in zip(splits, dtypes)]
    out_specs = [pl.BlockSpec((tm, hi - lo), lambda i: (i, 0)) for (lo, hi, _) in splits]
    return pl.pallas_call(
        functools.partial(_norm_matmul_kernel, splits=splits),
        grid=(n // tm,),
        in_specs=[pl.BlockSpec((tm, d), lambda i: (i, 0)),
                  pl.BlockSpec((1, d), lambda i: (0, 0)),
                  pl.BlockSpec((d, m), lambda i: (0, 0))],
        out_specs=out_specs,
        out_shape=out_shape,
        compiler_params=_cparams(("parallel",)),
        name="norm_matmul",
    )(x, g.reshape(1, d), w)


def _even_mixer_kernel(p_ref, bq_ref, are_ref, aim_ref, apr_ref, api_ref, cq_ref, d_ref, gw_ref, gb_ref, cw_ref,
                       ya_ref, yb_ref, bus_ref, hst_ref, zs_ref):
    t_len = p_ref.shape[0]
    aw = ya_ref.shape[1]
    n_pair = aw // LANES
    seg = t_len // S5_SEGMENTS

    @pl.when(pl.program_id(1) == 0)
    def _():
        hst_ref[...] = jnp.zeros_like(hst_ref)
        zs_ref[0:8, :] = jnp.zeros((8, zs_ref.shape[1]), F32)

    u = p_ref[:, 0:aw]
    ub = u.astype(BF16)
    sub = lax.broadcasted_iota(jnp.int32, (S5_SEGMENTS, LANES), 0)
    ys = []
    zero = jnp.zeros((S5_SEGMENTS, LANES), F32)
    rows_at = lambda t: pl.ds(t, S5_SEGMENTS, stride=S5_PITCH)
    for q_lo in range(0, S5_QUARTERS, S5_TOGETHER):
        quarters = range(q_lo, q_lo + S5_TOGETHER)
        slabs = [(q, (q - q_lo) * 2 * n_pair + j, (q - q_lo) * 2 * n_pair + n_pair + j)
                 for q in quarters for j in range(n_pair)]
        for q in quarters:
            bu = _dot(ub[:, q * LANES:(q + 1) * LANES], bq_ref[q])
            for j in range(2 * n_pair):
                for sg in range(S5_SEGMENTS):
                    bus_ref[(q - q_lo) * 2 * n_pair + j, sg * S5_PITCH:sg * S5_PITCH + seg, :] = (
                        bu[sg * seg:(sg + 1) * seg, j * LANES:(j + 1) * LANES])

        def a_rows(ref, q, j):
            return ref[q:q + 1, (j % n_pair) * LANES:(j % n_pair + 1) * LANES]

        def scan_step(store):
            def step(t, carry, slabs=slabs):
                out = []
                for i, (q, s_re, s_im) in enumerate(slabs):
                    hr, hi = carry[2 * i], carry[2 * i + 1]
                    ar = a_rows(are_ref, q, i)
                    ai = a_rows(aim_ref, q, i)
                    nr = ar * hr - ai * hi + bus_ref[s_re, rows_at(t), :]
                    ni = ar * hi + ai * hr + bus_ref[s_im, rows_at(t), :]
                    if store:
                        bus_ref[s_re, rows_at(t), :] = nr
                        bus_ref[s_im, rows_at(t), :] = ni
                    out += [nr, ni]
                return tuple(out)
            return step

        ends = lax.fori_loop(0, seg, scan_step(False), (zero,) * (2 * len(slabs)), unroll=2)

        h_in = []
        for i, (q, s_re, s_im) in enumerate(slabs):
            j = i % n_pair
            pr = apr_ref[q * n_pair + j, :, :]
            pi = api_ref[q * n_pair + j, :, :]
            cr = hst_ref[2 * q:2 * q + 1, j * LANES:(j + 1) * LANES]
            ci = hst_ref[2 * q + 1:2 * q + 2, j * LANES:(j + 1) * LANES]
            hr_in, hi_in = zero, zero
            for sg in range(S5_SEGMENTS):
                hr_in = jnp.where(sub == sg, cr, hr_in)
                hi_in = jnp.where(sub == sg, ci, hi_in)
                er = ends[2 * i][sg:sg + 1, :]
                ei = ends[2 * i + 1][sg:sg + 1, :]
                cr, ci = pr * cr - pi * ci + er, pr * ci + pi * cr + ei
            hst_ref[2 * q:2 * q + 1, j * LANES:(j + 1) * LANES] = cr
            hst_ref[2 * q + 1:2 * q + 2, j * LANES:(j + 1) * LANES] = ci
            h_in += [hr_in, hi_in]

        lax.fori_loop(0, seg, scan_step(True), tuple(h_in), unroll=2)

        for q in quarters:
            cols = []
            for j in range(2 * n_pair):
                cols.append(jnp.concatenate(
                    [bus_ref[(q - q_lo) * 2 * n_pair + j, sg * S5_PITCH:sg * S5_PITCH + seg, :]
                     for sg in range(S5_SEGMENTS)], axis=0))
            ys.append(_dot(jnp.concatenate(cols, axis=1).astype(BF16), cq_ref[q]))

    y = jnp.concatenate(ys, axis=1) + d_ref[...] * u
    y = _gelu(y)
    gate = jax.nn.sigmoid(_dot(y.astype(BF16), gw_ref[...]) + gb_ref[...])
    ya_ref[...] = (y * gate).astype(ya_ref.dtype)

    bw = yb_ref.shape[1]
    z = p_ref[:, aw + bw:aw + 2 * bw] * p_ref[:, aw + 2 * bw:aw + 3 * bw]
    zs_ref[8:8 + t_len, :] = z
    conv = (cw_ref[2:3, :] * z + cw_ref[1:2, :] * zs_ref[7:7 + t_len, :]
            + cw_ref[0:1, :] * zs_ref[6:6 + t_len, :])
    yb_ref[...] = (p_ref[:, aw:aw + bw] * conv).astype(yb_ref.dtype)
    zs_ref[0:8, :] = zs_ref[t_len:t_len + 8, :]


def _s5_params(lam_re, lam_im, log_dt, b_re, b_im, c_re, c_im, n_pow):
    g, p = lam_re.shape
    gq = g // S5_QUARTERS
    dt = jnp.exp(log_dt.astype(F32))[:, None]
    lr = lam_re.astype(F32)
    li = lam_im.astype(F32)
    mag = jnp.exp(lr * dt)
    ar = mag * jnp.cos(li * dt)
    ai = mag * jnp.sin(li * dt)
    den = lr * lr + li * li
    nr = ar - 1.0
    fr = (nr * lr + ai * li) / den
    fi = (ai * lr - nr * li) / den
    bre = b_re.astype(F32)
    bim = b_im.astype(F32)
    bbar_re = fr[..., None] * bre - fi[..., None] * bim
    bbar_im = fr[..., None] * bim + fi[..., None] * bre
    eye = jnp.eye(gq, dtype=F32)

    def in_blockdiag(m):
        m = m.reshape(S5_QUARTERS, gq, p, S5_GROUP)
        return jnp.einsum('qgpc,gh->qgchp', m, eye).reshape(S5_QUARTERS, gq * S5_GROUP, gq * p)

    def out_blockdiag(m):
        m = m.reshape(S5_QUARTERS, gq, S5_GROUP, p)
        return jnp.einsum('qgcp,gh->qgphc', m, eye).reshape(S5_QUARTERS, gq * p, gq * S5_GROUP)

    bq = jnp.concatenate([in_blockdiag(bbar_re), in_blockdiag(bbar_im)], axis=2).astype(BF16)
    cq = jnp.concatenate([out_blockdiag(c_re.astype(F32)), -out_blockdiag(c_im.astype(F32))],
                         axis=1).astype(BF16)
    are = ar.reshape(S5_QUARTERS, gq * p)
    aim = ai.reshape(S5_QUARTERS, gq * p)

    def next_power(_, c):
        cr, ci = c
        return cr * ar - ci * ai, cr * ai + ci * ar

    pw_re, pw_im = lax.fori_loop(1, n_pow, next_power, (ar, ai))
    slabs = lambda x: x.reshape(g * p // LANES, 1, LANES)
    return bq, are, aim, cq, slabs(pw_re), slabs(pw_im)


def _even_mixer(p, batch, bq, are, aim, apr, api, cq, d_skip, glu_w, glu_b, conv_w):
    n, width = p.shape
    seq = n // batch
    aw = d_skip.size
    bw = conv_w.shape[1]
    t = min(T_S5, seq)
    nt = seq // t
    qs = 2 * aw
    full = lambda shape: pl.BlockSpec(shape, lambda b, i: (0,) * len(shape))
    return pl.pallas_call(
        _even_mixer_kernel,
        grid=(batch, nt),
        in_specs=[pl.BlockSpec((t, width), lambda b, i: (b * nt + i, 0)),
                  full(bq.shape), full(are.shape), full(aim.shape), full(apr.shape), full(api.shape), full(cq.shape),
                  full((1, aw)), full(glu_w.shape), full((1, aw)), full(conv_w.shape)],
        out_specs=[pl.BlockSpec((t, aw), lambda b, i: (b * nt + i, 0)),
                   pl.BlockSpec((t, bw), lambda b, i: (b * nt + i, 0))],
        out_shape=[jax.ShapeDtypeStruct((n, aw), BF16), jax.ShapeDtypeStruct((n, bw), BF16)],
        scratch_shapes=[pltpu.VMEM((S5_TOGETHER * qs // LANES, S5_SEGMENTS * S5_PITCH, LANES), F32),
                        pltpu.VMEM((2 * S5_QUARTERS, aw), F32),
                        pltpu.VMEM((t + 8, bw), F32)],
        compiler_params=_cparams(("arbitrary", "arbitrary")),
        name="even_mixer",
    )(p, bq, are, aim, apr, api, cq, d_skip.reshape(1, aw).astype(F32), glu_w.astype(BF16),
      glu_b.reshape(1, aw).astype(F32), conv_w.astype(F32))


def _gmlp_kernel(uv_ref, g_ref, wm_ref, bias_ref, y_ref):
    cw = y_ref.shape[1]
    t_len = y_ref.shape[0]
    u = _gelu(uv_ref[:, 0:cw])
    v = _rms(_gelu(uv_ref[:, cw:2 * cw]), g_ref[...]).astype(BF16)
    lane = lax.broadcasted_iota(jnp.int32, (GMLP_CHUNK, LANES), 1)
    first_head = lane < GMLP_HEAD_DIM
    for c in range(t_len // GMLP_CHUNK):
        r0 = c * GMLP_CHUNK
        tiles = []
        for j in range(cw // LANES):
            vc = v[r0:r0 + GMLP_CHUNK, j * LANES:(j + 1) * LANES]
            tiles.append(jnp.where(first_head, _dot(wm_ref[2 * j], vc), _dot(wm_ref[2 * j + 1], vc)))
        y = jnp.concatenate(tiles, axis=1) + bias_ref[...]
        y_ref[r0:r0 + GMLP_CHUNK, :] = (u[r0:r0 + GMLP_CHUNK, :] * y).astype(y_ref.dtype)


def _gmlp(uv, vnorm_g, ws, bs):
    n = uv.shape[0]
    cw = uv.shape[1] // 2
    heads = ws.shape[0]
    t = min(T_GMLP, n)
    tril = jnp.tril(jnp.ones((GMLP_CHUNK, GMLP_CHUNK), dtype=bool))
    wm = jnp.where(tril, ws, jnp.zeros_like(ws)).astype(BF16)
    bias = jnp.repeat(bs.T.astype(F32), cw // heads, axis=1)
    return pl.pallas_call(
        _gmlp_kernel,
        grid=(n // t,),
        in_specs=[pl.BlockSpec((t, 2 * cw), lambda i: (i, 0)),
                  pl.BlockSpec((1, cw), lambda i: (0, 0)),
                  pl.BlockSpec(wm.shape, lambda i: (0, 0, 0)),
                  pl.BlockSpec(bias.shape, lambda i: (0, 0))],
        out_specs=pl.BlockSpec((t, cw), lambda i: (i, 0)),
        out_shape=jax.ShapeDtypeStruct((n, cw), BF16),
        compiler_params=_cparams(("parallel",)),
        name="gmlp",
    )(uv, vnorm_g.reshape(1, cw).astype(F32), wm, bias)


def _sb_attn_kernel(q_ref, k_ref, v_ref, tri_ref, o_ref, acc_ref, run_ref):
    qi = pl.program_id(2)
    n_sub = q_ref.shape[0] // TK_SB
    q = q_ref[...]
    lane = lax.broadcasted_iota(jnp.int32, q.shape, 1)
    lane_sub = lax.broadcasted_iota(jnp.int32, (TK_SB, LANES), 1)
    q_h0 = jnp.where(lane < SB_HEAD_DIM, q, jnp.zeros_like(q))
    q_h1 = jnp.where(lane < SB_HEAD_DIM, jnp.zeros_like(q), q)
    row = lax.broadcasted_iota(jnp.int32, (2 * TK_SB, TK_SB), 0)
    col = lax.broadcasted_iota(jnp.int32, (2 * TK_SB, TK_SB), 1)
    below_diag = col < jnp.where(row >= TK_SB, row - TK_SB, row)

    def visit(qs, kb, run, acc, diag):
        okf = jnp.where(kb >= 0, 1.0, 0.0).astype(F32)
        k0 = pl.multiple_of(jnp.maximum(kb, 0) * TK_SB, TK_SB)
        kblk = k_ref[pl.ds(k0, TK_SB), :]
        vblk = v_ref[pl.ds(k0, TK_SB), :]
        z = _dot_nt(qs, kblk)
        sp = jnp.maximum(z, 0.0) + jnp.log(1.0 + jnp.exp(-jnp.abs(z)))
        lm = jnp.where(below_diag, -sp, 0.0) if diag else -sp * okf
        lm_hi = lm.astype(BF16)
        lm_lo = (lm - lm_hi.astype(F32)).astype(BF16)
        cs = _dot(jnp.concatenate([lm_hi, lm_lo], axis=0), tri_ref[...])
        cs = cs[:2 * TK_SB] + cs[2 * TK_SB:]
        w = jnp.exp(z - sp + cs[:, :TK_SB] + run)
        w = jnp.where(below_diag, w, 0.0) if diag else w * okf
        return run + cs[:, TK_SB:], acc + _dot(w.astype(BF16), vblk)

    def stacked_q(sub):
        rows = slice(sub * TK_SB, (sub + 1) * TK_SB)
        return jnp.concatenate([q_h0[rows], q_h1[rows]], axis=0)

    live = False
    for sub in range(n_sub):
        qs = stacked_q(sub)
        run = jnp.zeros((2 * TK_SB, TK_SB), F32)
        acc = jnp.zeros((2 * TK_SB, LANES), F32)
        for o in range(SB_STATIC_BLOCKS):
            run, acc = visit(qs, qi * n_sub + sub - o, run, acc, o == 0)
        run_ref[sub] = run
        acc_ref[sub] = acc
        live = jnp.logical_or(live, jnp.max(run) > SB_LOG_CUTOFF)

    def cond(c):
        o, live = c
        return jnp.logical_and(live, o <= qi * n_sub + n_sub - 1)

    def body(c):
        o, _ = c
        live = False
        for sub in range(n_sub):
            run, acc = visit(stacked_q(sub), qi * n_sub + sub - o, run_ref[sub], acc_ref[sub], False)
            run_ref[sub] = run
            acc_ref[sub] = acc
            live = jnp.logical_or(live, jnp.max(run) > SB_LOG_CUTOFF)
        return o + 1, live

    lax.while_loop(cond, body, (SB_STATIC_BLOCKS, live))
    for sub in range(n_sub):
        acc = acc_ref[sub]
        rows = slice(sub * TK_SB, (sub + 1) * TK_SB)
        o_ref[rows, :] = jnp.where(lane_sub < SB_HEAD_DIM, acc[:TK_SB], acc[TK_SB:]).astype(o_ref.dtype)


def _sb_attention(q, k, v, batch):
    n, dw = q.shape
    seq = n // batch
    tq = min(TQ_SB, seq)
    nq = seq // tq
    npair = dw // LANES
    s_idx = jnp.arange(TK_SB)[:, None]
    j_idx = jnp.arange(TK_SB)[None, :]
    tri = jnp.concatenate([(s_idx > j_idx), jnp.ones((TK_SB, TK_SB), bool)], axis=1).astype(BF16)
    return pl.pallas_call(
        _sb_attn_kernel,
        grid=(batch, npair, nq),
        in_specs=[pl.BlockSpec((tq, LANES), lambda b, h, i: (b * nq + i, h)),
                  pl.BlockSpec((seq, LANES), lambda b, h, i: (b, h)),
                  pl.BlockSpec((seq, LANES), lambda b, h, i: (b, h)),
                  pl.BlockSpec(tri.shape, lambda b, h, i: (0, 0))],
        out_specs=pl.BlockSpec((tq, LANES), lambda b, h, i: (b * nq + i, h)),
        out_shape=jax.ShapeDtypeStruct((n, dw), BF16),
        scratch_shapes=[pltpu.VMEM((tq // TK_SB, 2 * TK_SB, LANES), F32),
                        pltpu.VMEM((tq // TK_SB, 2 * TK_SB, TK_SB), F32)],
        compiler_params=_cparams(("parallel", "parallel", "arbitrary")),
        name="sb_attn",
    )(q, k, v, tri)


def _sort_pairs(n):
    pairs = []
    p = 1
    while p < n:
        k = p
        while k >= 1:
            for j in range(k % p, n - k, 2 * k):
                for i in range(min(k, n - j - k)):
                    if (i + j) // (2 * p) == (i + j + k) // (2 * p):
                        pairs.append((i + j, i + j + k))
            k //= 2
        p *= 2
    return pairs


def _sublane_all(x, op):
    for shift in (4, 2, 1):
        x = op(x, pltpu.roll(x, shift, 0))
    return x


def _top16(tiles):
    t = list(tiles)
    for i, j in _sort_pairs(len(t)):
        t[i], t[j] = jnp.maximum(t[i], t[j]), jnp.minimum(t[i], t[j])
    n = len(t)
    for shift in (4, 2, 1):
        other = [pltpu.roll(x, shift, 0) for x in t]
        t = [jnp.maximum(t[i], other[n - 1 - i]) for i in range(n)]
        d = n // 2
        while d >= 1:
            for i in range(n):
                if not i & d:
                    t[i], t[i + d] = jnp.maximum(t[i], t[i + d]), jnp.minimum(t[i], t[i + d])
            d //= 2
    return t


def _next_below(tiles, bound):
    m = None
    for x in tiles:
        y = jnp.where(x < bound, x, NEG_INF)
        m = y if m is None else jnp.maximum(m, y)
    return _sublane_all(m, jnp.maximum)


def _rows(tiles):
    sub = lax.broadcasted_iota(jnp.int32, tiles[0].shape, 0)
    out = tiles[-1]
    for r in range(len(tiles) - 2, -1, -1):
        out = jnp.where(sub == r, tiles[r], out)
    return out


def _peer_select_kernel(x_ref, ya_ref, yb_ref, wo_ref, g_ref, wq_ref, k1_ref, k2_ref,
                        h_ref, hx_ref, th_ref, e1_ref, s2_ref, e2_ref, s1_ref, s2s_ref, q_ref):
    t_len = x_ref.shape[0]
    wa = ya_ref.shape[1]
    keys = s1_ref.shape[0]

    @pl.when(pl.program_id(1) == 0)
    def _():
        mix = _dot(ya_ref[...], wo_ref[0:wa, :]) + _dot(yb_ref[...], wo_ref[wa:, :])
        h = x_ref[...] + mix
        h_ref[...] = h
        hx = _rms(h, g_ref[...]).astype(BF16)
        hx_ref[...] = hx
        q_ref[...] = _dot(hx, wq_ref[...]).astype(BF16)

    half = k1_ref.shape[2]
    q0 = pl.multiple_of(pl.program_id(1) * (2 * half), 2 * half)
    s1_ref[...] = _dot_nt(k1_ref[0], q_ref[:, pl.ds(q0, half)])
    s2s_ref[...] = _dot_nt(k2_ref[0], q_ref[:, pl.ds(q0 + half, half)])

    i8 = lax.broadcasted_iota(jnp.int32, (8, LANES), 0)
    neg = jnp.full((8, LANES), NEG_INF, F32)

    def chunk(c, carry):
        l0 = pl.multiple_of(c * LANES, LANES)
        s1 = [s1_ref[r:r + 8, pl.ds(l0, LANES)] for r in range(0, keys, 8)]
        s2 = [s2s_ref[r:r + 8, pl.ds(l0, LANES)] for r in range(0, keys, 8)]
        v1 = _top16(s1)
        v2 = _top16(s2)
        v1_17 = _next_below(s1, v1[15])
        v2_17 = _next_below(s2, v2[15])
        v2_lo = _rows(v2[0:8])
        v2_hi = _rows(v2[8:16])
        cands = [v1[0] + v2_lo, v1[0] + v2_hi, v1[1] + v2_lo]
        for a, nb in ((2, 5), (3, 4), (4, 3), (5, 2), (6, 2), (7, 2)):
            cands.append(jnp.where(i8 < nb, v1[a] + v2_lo, NEG_INF))
        cands.append(_rows(v1[8:16]) + v2[0])
        cands.append(jnp.where(i8 == 0, v1[0] + v2_17, jnp.where(i8 == 1, v1_17 + v2[0], NEG_INF)))
        top = _top16(cands + [neg] * (16 - len(cands)))
        t16 = top[15]
        t17 = _next_below(cands, t16)
        tau = jnp.where(t17 == NEG_INF, t16, 0.5 * (t16 + t17))
        m1 = v1[0]
        m2 = v2[0]
        zs = None
        for x in cands:
            y = jnp.where(x >= tau, jnp.exp(x - (m1 + m2)), 0.0)
            zs = y if zs is None else zs + y
        scale = 0.5 / _sublane_all(zs, jnp.add)
        for i in range(keys // 8):
            rows = slice(8 * i, 8 * i + 8)
            th_ref[0, c, rows, :] = tau - s1[i]
            e1_ref[0, c, rows, :] = jnp.exp(s1[i] - m1)
            s2_ref[0, c, rows, :] = s2[i]
            e2_ref[0, c, rows, :] = jnp.exp(s2[i] - m2) * scale
        return carry

    lax.fori_loop(0, t_len // LANES, chunk, 0, unroll=2)


def _peer_select(x, ya, yb, w_out, g, wq, k1, k2):
    n, d = x.shape
    wa = ya.shape[1]
    heads, keys, half = k1.shape
    t = min(T_PEER, n)
    sel_shape = jax.ShapeDtypeStruct((heads, n // LANES, keys, LANES), F32)
    sel_spec = pl.BlockSpec((1, t // LANES, keys, LANES), lambda i, h: (h, i, 0, 0))
    return pl.pallas_call(
        _peer_select_kernel,
        grid=(n // t, heads),
        in_specs=[pl.BlockSpec((t, d), lambda i, h: (i, 0)),
                  pl.BlockSpec((t, wa), lambda i, h: (i, 0)),
                  pl.BlockSpec((t, yb.shape[1]), lambda i, h: (i, 0)),
                  pl.BlockSpec(w_out.shape, lambda i, h: (0, 0)),
                  pl.BlockSpec((1, d), lambda i, h: (0, 0)),
                  pl.BlockSpec(wq.shape, lambda i, h: (0, 0)),
                  pl.BlockSpec((1, keys, half), lambda i, h: (h, 0, 0)),
                  pl.BlockSpec((1, keys, half), lambda i, h: (h, 0, 0))],
        out_specs=[pl.BlockSpec((t, d), lambda i, h: (i, 0)),
                   pl.BlockSpec((t, d), lambda i, h: (i, 0)),
                   sel_spec, sel_spec, sel_spec, sel_spec],
        out_shape=[jax.ShapeDtypeStruct((n, d), F32), jax.ShapeDtypeStruct((n, d), BF16),
                   sel_shape, sel_shape, sel_shape, sel_shape],
        scratch_shapes=[pltpu.VMEM((keys, t), F32), pltpu.VMEM((keys, t), F32),
                        pltpu.VMEM((t, wq.shape[1]), BF16)],
        compiler_params=_cparams(("parallel", "arbitrary")),
        name="peer_select",
    )(x, ya, yb, w_out.astype(BF16), g.reshape(1, d).astype(F32), wq.astype(BF16),
      k1.astype(BF16), k2.astype(BF16))


def _peer_dense_kernel(hx_ref, th_ref, e1_ref, s2_ref, e2_ref, u_ref, vt_ref, res_ref, gf_ref,
                       out_ref, acc_ref, act0_ref, act1_ref, w0_ref, w1_ref, *, final_norm):
    j = pl.program_id(1)
    n_steps = pl.num_programs(1)
    heads, n_lt, keys, _ = th_ref.shape
    t_len = n_lt * LANES
    e_sub = act0_ref.shape[0]
    n_i1 = e_sub // keys
    d_rows = acc_ref.shape[0] // n_i1
    m_rows = 2 * e_sub // n_i1
    tw = t_len // 2
    rc = 32
    c0 = math.sqrt(2.0 / math.pi)
    c1 = c0 * 0.044715

    never = lax.broadcasted_iota(jnp.int32, (rc, LANES), 0) < jnp.minimum(j, 0)

    def slot_loop(sub, do1, do2, do3):
        act_w, act_r = (act1_ref, act0_ref) if sub else (act0_ref, act1_ref)
        w_w, w_r = (w0_ref, w1_ref) if sub else (w1_ref, w0_ref)
        i1_base = (2 * j + sub - 1) * n_i1

        def body(k, carry):
            deps = []
            if do3:
                d0 = pl.multiple_of(k * d_rows, d_rows)
                p3 = _dot(vt_ref[pl.ds(d0, d_rows), sub * e_sub:(sub + 1) * e_sub], w_r[...])
                acc_ref[pl.ds(d0, d_rows), :] += p3
                deps.append([p3[c * (d_rows // n_lt):c * (d_rows // n_lt) + rc, 0:LANES] for c in range(n_lt)])
            if do1:
                m0 = pl.multiple_of((k // 2) * m_rows, m_rows)
                t0 = pl.multiple_of((k % 2) * tw, tw)
                p1 = _dot_nt(u_ref[pl.ds(sub * e_sub + m0, m_rows), :], hx_ref[pl.ds(t0, tw), :])
                act_w[pl.ds(m0, m_rows), pl.ds(t0, tw)] = p1
                deps.append([p1[c * (m_rows // n_lt):c * (m_rows // n_lt) + rc, 0:LANES] for c in range(n_lt)])
            if do2:
                i1 = i1_base + k
                r0 = pl.multiple_of(k * keys, keys)
                for lt in range(n_lt):
                    ls = slice(lt * LANES, (lt + 1) * LANES)
                    th = [jnp.broadcast_to(th_ref[h, lt, pl.ds(i1, 1), :], (rc, LANES)) for h in range(heads)]
                    e1 = [jnp.broadcast_to(e1_ref[h, lt, pl.ds(i1, 1), :], (rc, LANES)) for h in range(heads)]
                    for r in range(0, keys, rc):
                        g = jnp.zeros((rc, LANES), F32)
                        for h in range(heads):
                            g = g + jnp.where(s2_ref[h, lt, r:r + rc, :] >= th[h], e2_ref[h, lt, r:r + rc, :], 0.0) * e1[h]
                        a = act_r[pl.ds(r0 + r, rc), ls]
                        tanh = jnp.tanh(a * (c0 + c1 * (a * a)))
                        ga = g * a
                        if r == keys - rc:
                            for dep in deps:
                                ga = jnp.where(never, dep[lt], ga)
                        w_w[pl.ds(r0 + r, rc), ls] = (ga + ga * tanh).astype(BF16)
            return carry

        lax.fori_loop(0, n_i1, body, 0, unroll=2)

    @pl.when(j == 0)
    def _():
        acc_ref[...] = jnp.zeros_like(acc_ref)
        slot_loop(0, True, False, False)
        slot_loop(1, True, True, False)

    @pl.when(jnp.logical_and(j > 0, j < n_steps - 1))
    def _():
        slot_loop(0, True, True, True)
        slot_loop(1, True, True, True)

    @pl.when(j == n_steps - 1)
    def _():
        slot_loop(0, False, True, True)
        slot_loop(1, False, False, True)
        o = res_ref[...] + acc_ref[...].T
        if final_norm:
            o = _rms(o, gf_ref[...])
        out_ref[...] = o


def _peer_dense(hx, th, e1, s2, e2, u_tab, vt_tab, res, g_final, final_norm):
    n, d = res.shape
    heads, _, keys, _ = th.shape
    n_exp = u_tab.shape[0]
    t = min(T_PEER, n)
    e_blk = min(E_BLK, n_exp)
    nj = n_exp // e_blk
    once = pl.Buffered(1)
    sel_spec = pl.BlockSpec((heads, t // LANES, keys, LANES), lambda i, j: (0, i, 0, 0), pipeline_mode=once)
    return pl.pallas_call(
        functools.partial(_peer_dense_kernel, final_norm=final_norm),
        grid=(n // t, nj + 1),
        in_specs=[pl.BlockSpec((t, d), lambda i, j: (i, 0), pipeline_mode=once),
                  sel_spec, sel_spec, sel_spec, sel_spec,
                  pl.BlockSpec((e_blk, d), lambda i, j: (jnp.minimum(j, nj - 1), 0)),
                  pl.BlockSpec((d, e_blk), lambda i, j: (0, jnp.maximum(j - 1, 0))),
                  pl.BlockSpec((t, d), lambda i, j: (i, 0), pipeline_mode=once),
                  pl.BlockSpec((1, d), lambda i, j: (0, 0))],
        out_specs=pl.BlockSpec((t, d), lambda i, j: (i, 0)),
        out_shape=jax.ShapeDtypeStruct((n, d), F32),
        scratch_shapes=[pltpu.VMEM((d, t), F32),
                        pltpu.VMEM((e_blk // 2, t), F32), pltpu.VMEM((e_blk // 2, t), F32),
                        pltpu.VMEM((e_blk // 2, t), BF16), pltpu.VMEM((e_blk // 2, t), BF16)],
        compiler_params=_cparams(("parallel", "arbitrary")),
        name="peer_dense",
    )(hx, th, e1, s2, e2, u_tab, vt_tab, res, g_final.reshape(1, d).astype(F32))


def _peer_block(x, ya, yb, w_out, g_ffn, wq, k1, k2, u_tab, v_tab, g_final, final_norm):
    h, hx, th, e1, s2, e2 = _peer_select(x, ya, yb, w_out, g_ffn, wq, k1, k2)
    return _peer_dense(hx, th, e1, s2, e2, u_tab.astype(BF16), v_tab.astype(BF16).T, h, g_final,
                       final_norm)


def kernel(x, l0_norm_mix_g, l0_w_in, a_lam_re, a_lam_im, a_log_dt, a_b_re, a_b_im, a_c_re, a_c_im, a_d, a_glu_w, a_glu_b, b_conv_w, l0_w_out, l0_norm_ffn_g, l0_peer_wq, l0_peer_k1, l0_peer_k2, l0_peer_u, l0_peer_v, l1_norm_mix_g, l1_w_in, c_vnorm_g, c_ws, c_bs, l1_w_out, l1_norm_ffn_g, l1_peer_wq, l1_peer_k1, l1_peer_k2, l1_peer_u, l1_peer_v, final_norm_g):
    batch, seq, d = x.shape
    n = batch * seq
    h = x.reshape(n, d)

    (p0,) = _norm_matmul(h, l0_norm_mix_g.astype(F32), l0_w_in.astype(BF16),
                         ((0, l0_w_in.shape[1], 1.0),), (F32,))
    bq, are, aim, cq, apr, api = _s5_params(a_lam_re, a_lam_im, a_log_dt, a_b_re, a_b_im, a_c_re, a_c_im,
                                            min(T_S5, seq) // S5_SEGMENTS)
    ya, yb = _even_mixer(p0, batch, bq, are, aim, apr, api, cq, a_d, a_glu_w, a_glu_b, b_conv_w)
    h = _peer_block(h, ya, yb, l0_w_out, l0_norm_ffn_g, l0_peer_wq, l0_peer_k1, l0_peer_k2,
                    l0_peer_u, l0_peer_v, final_norm_g, False)

    cw = c_vnorm_g.shape[0]
    dw = (l1_w_in.shape[1] - 2 * cw) // 3
    o = 2 * cw
    splits = ((0, o, 1.0), (o, o + dw, SB_HEAD_DIM ** -0.5), (o + dw, o + 2 * dw, 1.0),
              (o + 2 * dw, o + 3 * dw, 1.0))
    uv, q, k, v = _norm_matmul(h, l1_norm_mix_g.astype(F32), l1_w_in.astype(BF16), splits,
                               (F32, BF16, BF16, BF16))
    yc = _gmlp(uv, c_vnorm_g, c_ws, c_bs)
    yd = _sb_attention(q, k, v, batch)
    h = _peer_block(h, yc, yd, l1_w_out, l1_norm_ffn_g, l1_peer_wq, l1_peer_k1, l1_peer_k2,
                    l1_peer_u, l1_peer_v, final_norm_g, True)
    return h.reshape(batch, seq, d)
```

```python
import functools
import math

import jax
import jax.numpy as jnp
from jax import lax
from jax.experimental import pallas as pl
from jax.experimental.pallas import tpu as pltpu

F32 = jnp.float32
BF16 = jnp.bfloat16
NEG_INF = float("-inf")

RMS_EPS = 1e-6
LANES = 128
S5_GROUP = 16
S5_QUARTERS = 4
S5_SEGMENTS = 8
S5_TOGETHER = 2
S5_PITCH = 40
PEER_KEYS = 128
PEER_TOPK = 16
SB_HEAD_DIM = 64
GMLP_CHUNK = 128
GMLP_HEAD_DIM = 64
SB_LOG_CUTOFF = -104.0

TM_PROJ = 512
T_S5 = 256
T_GMLP = 512
TQ_SB = 512
TK_SB = 128
SB_STATIC_BLOCKS = 3
T_PEER = 1024
E_BLK = 1024
VMEM_LIMIT = 58 * 1024 * 1024


def _cparams(sem, flags=None):
    return pltpu.CompilerParams(dimension_semantics=sem, vmem_limit_bytes=VMEM_LIMIT, flags=flags)


def _dot(a, b):
    return jnp.dot(a, b, preferred_element_type=F32)


def _dot_nt(a, b):
    return lax.dot_general(a, b, (((1,), (1,)), ((), ())), preferred_element_type=F32)


def _gelu(x):
    c = math.sqrt(2.0 / math.pi)
    return 0.5 * x * (1.0 + jnp.tanh(c * (x + 0.044715 * (x * x * x))))


def _rms(x, g):
    return x * lax.rsqrt(jnp.mean(x * x, axis=-1, keepdims=True) + RMS_EPS) * g


def _norm_matmul_kernel(x_ref, g_ref, w_ref, *out_refs, splits):
    xn = _rms(x_ref[...], g_ref[...]).astype(BF16)
    p = _dot(xn, w_ref[...])
    for o_ref, (lo, hi, scale) in zip(out_refs, splits):
        part = p[:, lo:hi]
        if scale != 1.0:
            part = part * scale
        o_ref[...] = part.astype(o_ref.dtype)


def _norm_matmul(x, g, w, splits, dtypes):
    n, d = x.shape
    m = w.shape[1]
    tm = min(TM_PROJ, n)
    out_shape = [jax.ShapeDtypeStruct((n, hi - lo), dt) for (lo, hi, _), dt in zip(splits, dtypes)]
    out_specs = [pl.BlockSpec((tm, hi - lo), lambda i: (i, 0)) for (lo, hi, _) in splits]
    return pl.pallas_call(
        functools.partial(_norm_matmul_kernel, splits=splits),
        grid=(n // tm,),
        in_specs=[pl.BlockSpec((tm, d), lambda i: (i, 0)),
                  pl.BlockSpec((1, d), lambda i: (0, 0)),
                  pl.BlockSpec((d, m), lambda i: (0, 0))],
        out_specs=out_specs,
        out_shape=out_shape,
        compiler_params=_cparams(("parallel",)),
        name="norm_matmul",
    )(x, g.reshape(1, d), w)


def _even_mixer_kernel(p_ref, bq_ref, are_ref, aim_ref, apr_ref, api_ref, cq_ref, d_ref, gw_ref, gb_ref, cw_ref,
                       ya_ref, yb_ref, bus_ref, hst_ref, zs_ref):
    t_len = p_ref.shape[0]
    aw = ya_ref.shape[1]
    n_pair = aw // LANES
    seg = t_len // S5_SEGMENTS

    @pl.when(pl.program_id(1) == 0)
    def _():
        hst_ref[...] = jnp.zeros_like(hst_ref)
        zs_ref[0:8, :] = jnp.zeros((8, zs_ref.shape[1]), F32)

    u = p_ref[:, 0:aw]
    ub = u.astype(BF16)
    sub = lax.broadcasted_iota(jnp.int32, (S5_SEGMENTS, LANES), 0)
    ys = []
    zero = jnp.zeros((S5_SEGMENTS, LANES), F32)
    rows_at = lambda t: pl.ds(t, S5_SEGMENTS, stride=S5_PITCH)
    for q_lo in range(0, S5_QUARTERS, S5_TOGETHER):
        quarters = range(q_lo, q_lo + S5_TOGETHER)
        slabs = [(q, (q - q_lo) * 2 * n_pair + j, (q - q_lo) * 2 * n_pair + n_pair + j)
                 for q in quarters for j in range(n_pair)]
        for q in quarters:
            bu = _dot(ub[:, q * LANES:(q + 1) * LANES], bq_ref[q])
            for j in range(2 * n_pair):
                for sg in range(S5_SEGMENTS):
                    bus_ref[(q - q_lo) * 2 * n_pair + j, sg * S5_PITCH:sg * S5_PITCH + seg, :] = (
                        bu[sg * seg:(sg + 1) * seg, j * LANES:(j + 1) * LANES])

        def a_rows(ref, q, j):
            return ref[q:q + 1, (j % n_pair) * LANES:(j % n_pair + 1) * LANES]

        def scan_step(store):
            def step(t, carry, slabs=slabs):
                out = []
                for i, (q, s_re, s_im) in enumerate(slabs):
                    hr, hi = carry[2 * i], carry[2 * i + 1]
                    ar = a_rows(are_ref, q, i)
                    ai = a_rows(aim_ref, q, i)
                    nr = ar * hr - ai * hi + bus_ref[s_re, rows_at(t), :]
                    ni = ar * hi + ai * hr + bus_ref[s_im, rows_at(t), :]
                    if store:
                        bus_ref[s_re, rows_at(t), :] = nr
                        bus_ref[s_im, rows_at(t), :] = ni
                    out += [nr, ni]
                return tuple(out)
            return step

        ends = lax.fori_loop(0, seg, scan_step(False), (zero,) * (2 * len(slabs)), unroll=2)

        h_in = []
        for i, (q, s_re, s_im) in enumerate(slabs):
            j = i % n_pair
            pr = apr_ref[q * n_pair + j, :, :]
            pi = api_ref[q * n_pair + j, :, :]
            cr = hst_ref[2 * q:2 * q + 1, j * LANES:(j + 1) * LANES]
            ci = hst_ref[2 * q + 1:2 * q + 2, j * LANES:(j + 1) * LANES]
            hr_in, hi_in = zero, zero
            for sg in range(S5_SEGMENTS):
                hr_in = jnp.where(sub == sg, cr, hr_in)
                hi_in = jnp.where(sub == sg, ci, hi_in)
                er = ends[2 * i][sg:sg + 1, :]
                ei = ends[2 * i + 1][sg:sg + 1, :]
                cr, ci = pr * cr - pi * ci + er, pr * ci + pi * cr + ei
            hst_ref[2 * q:2 * q + 1, j * LANES:(j + 1) * LANES] = cr
            hst_ref[2 * q + 1:2 * q + 2, j * LANES:(j + 1) * LANES] = ci
            h_in += [hr_in, hi_in]

        lax.fori_loop(0, seg, scan_step(True), tuple(h_in), unroll=2)

        for q in quarters:
            cols = []
            for j in range(2 * n_pair):
                cols.append(jnp.concatenate(
                    [bus_ref[(q - q_lo) * 2 * n_pair + j, sg * S5_PITCH:sg * S5_PITCH + seg, :]
                     for sg in range(S5_SEGMENTS)], axis=0))
            ys.append(_dot(jnp.concatenate(cols, axis=1).astype(BF16), cq_ref[q]))

    y = jnp.concatenate(ys, axis=1) + d_ref[...] * u
    y = _gelu(y)
    gate = jax.nn.sigmoid(_dot(y.astype(BF16), gw_ref[...]) + gb_ref[...])
    ya_ref[...] = (y * gate).astype(ya_ref.dtype)

    bw = yb_ref.shape[1]
    z = p_ref[:, aw + bw:aw + 2 * bw] * p_ref[:, aw + 2 * bw:aw + 3 * bw]
    zs_ref[8:8 + t_len, :] = z
    conv = (cw_ref[2:3, :] * z + cw_ref[1:2, :] * zs_ref[7:7 + t_len, :]
            + cw_ref[0:1, :] * zs_ref[6:6 + t_len, :])
    yb_ref[...] = (p_ref[:, aw:aw + bw] * conv).astype(yb_ref.dtype)
    zs_ref[0:8, :] = zs_ref[t_len:t_len + 8, :]


def _s5_params(lam_re, lam_im, log_dt, b_re, b_im, c_re, c_im, n_pow):
    g, p = lam_re.shape
    gq = g // S5_QUARTERS
    dt = jnp.exp(log_dt.astype(F32))[:, None]
    lr = lam_re.astype(F32)
    li = lam_im.astype(F32)
    mag = jnp.exp(lr * dt)
    ar = mag * jnp.cos(li * dt)
    ai = mag * jnp.sin(li * dt)
    den = lr * lr + li * li
    nr = ar - 1.0
    fr = (nr * lr + ai * li) / den
    fi = (ai * lr - nr * li) / den
    bre = b_re.astype(F32)
    bim = b_im.astype(F32)
    bbar_re = fr[..., None] * bre - fi[..., None] * bim
    bbar_im = fr[..., None] * bim + fi[..., None] * bre
    eye = jnp.eye(gq, dtype=F32)

    def in_blockdiag(m):
        m = m.reshape(S5_QUARTERS, gq, p, S5_GROUP)
        return jnp.einsum('qgpc,gh->qgchp', m, eye).reshape(S5_QUARTERS, gq * S5_GROUP, gq * p)

    def out_blockdiag(m):
        m = m.reshape(S5_QUARTERS, gq, S5_GROUP, p)
        return jnp.einsum('qgcp,gh->qgphc', m, eye).reshape(S5_QUARTERS, gq * p, gq * S5_GROUP)

    bq = jnp.concatenate([in_blockdiag(bbar_re), in_blockdiag(bbar_im)], axis=2).astype(BF16)
    cq = jnp.concatenate([out_blockdiag(c_re.astype(F32)), -out_blockdiag(c_im.astype(F32))],
                         axis=1).astype(BF16)
    are = ar.reshape(S5_QUARTERS, gq * p)
    aim = ai.reshape(S5_QUARTERS, gq * p)

    def next_power(_, c):
        cr, ci = c
        return cr * ar - ci * ai, cr * ai + ci * ar

    pw_re, pw_im = lax.fori_loop(1, n_pow, next_power, (ar, ai))
    slabs = lambda x: x.reshape(g * p // LANES, 1, LANES)
    return bq, are, aim, cq, slabs(pw_re), slabs(pw_im)


def _even_mixer(p, batch, bq, are, aim, apr, api, cq, d_skip, glu_w, glu_b, conv_w):
    n, width = p.shape
    seq = n // batch
    aw = d_skip.size
    bw = conv_w.shape[1]
    t = min(T_S5, seq)
    nt = seq // t
    qs = 2 * aw
    full = lambda shape: pl.BlockSpec(shape, lambda b, i: (0,) * len(shape))
    return pl.pallas_call(
        _even_mixer_kernel,
        grid=(batch, nt),
        in_specs=[pl.BlockSpec((t, width), lambda b, i: (b * nt + i, 0)),
                  full(bq.shape), full(are.shape), full(aim.shape), full(apr.shape), full(api.shape), full(cq.shape),
                  full((1, aw)), full(glu_w.shape), full((1, aw)), full(conv_w.shape)],
        out_specs=[pl.BlockSpec((t, aw), lambda b, i: (b * nt + i, 0)),
                   pl.BlockSpec((t, bw), lambda b, i: (b * nt + i, 0))],
        out_shape=[jax.ShapeDtypeStruct((n, aw), BF16), jax.ShapeDtypeStruct((n, bw), BF16)],
        scratch_shapes=[pltpu.VMEM((S5_TOGETHER * qs // LANES, S5_SEGMENTS * S5_PITCH, LANES), F32),
                        pltpu.VMEM((2 * S5_QUARTERS, aw), F32),
                        pltpu.VMEM((t + 8, bw), F32)],
        compiler_params=_cparams(("arbitrary", "arbitrary")),
        name="even_mixer",
    )(p, bq, are, aim, apr, api, cq, d_skip.reshape(1, aw).astype(F32), glu_w.astype(BF16),
      glu_b.reshape(1, aw).astype(F32), conv_w.astype(F32))


def _gmlp_kernel(uv_ref, g_ref, wm_ref, bias_ref, y_ref):
    cw = y_ref.shape[1]
    t_len = y_ref.shape[0]
    u = _gelu(uv_ref[:, 0:cw])
    v = _rms(_gelu(uv_ref[:, cw:2 * cw]), g_ref[...]).astype(BF16)
    lane = lax.broadcasted_iota(jnp.int32, (GMLP_CHUNK, LANES), 1)
    first_head = lane < GMLP_HEAD_DIM
    for c in range(t_len // GMLP_CHUNK):
        r0 = c * GMLP_CHUNK
        tiles = []
        for j in range(cw // LANES):
            vc = v[r0:r0 + GMLP_CHUNK, j * LANES:(j + 1) * LANES]
            tiles.append(jnp.where(first_head, _dot(wm_ref[2 * j], vc), _dot(wm_ref[2 * j + 1], vc)))
        y = jnp.concatenate(tiles, axis=1) + bias_ref[...]
        y_ref[r0:r0 + GMLP_CHUNK, :] = (u[r0:r0 + GMLP_CHUNK, :] * y).astype(y_ref.dtype)


def _gmlp(uv, vnorm_g, ws, bs):
    n = uv.shape[0]
    cw = uv.shape[1] // 2
    heads = ws.shape[0]
    t = min(T_GMLP, n)
    tril = jnp.tril(jnp.ones((GMLP_CHUNK, GMLP_CHUNK), dtype=bool))
    wm = jnp.where(tril, ws, jnp.zeros_like(ws)).astype(BF16)
    bias = jnp.repeat(bs.T.astype(F32), cw // heads, axis=1)
    return pl.pallas_call(
        _gmlp_kernel,
        grid=(n // t,),
        in_specs=[pl.BlockSpec((t, 2 * cw), lambda i: (i, 0)),
                  pl.BlockSpec((1, cw), lambda i: (0, 0)),
                  pl.BlockSpec(wm.shape, lambda i: (0, 0, 0)),
                  pl.BlockSpec(bias.shape, lambda i: (0, 0))],
        out_specs=pl.BlockSpec((t, cw), lambda i: (i, 0)),
        out_shape=jax.ShapeDtypeStruct((n, cw), BF16),
        compiler_params=_cparams(("parallel",)),
        name="gmlp",
    )(uv, vnorm_g.reshape(1, cw).astype(F32), wm, bias)


def _sb_attn_kernel(q_ref, k_ref, v_ref, tri_ref, o_ref, acc_ref, run_ref):
    qi = pl.program_id(2)
    n_sub = q_ref.shape[0] // TK_SB
    q = q_ref[...]
    lane = lax.broadcasted_iota(jnp.int32, q.shape, 1)
    lane_sub = lax.broadcasted_iota(jnp.int32, (TK_SB, LANES), 1)
    q_h0 = jnp.where(lane < SB_HEAD_DIM, q, jnp.zeros_like(q))
    q_h1 = jnp.where(lane < SB_HEAD_DIM, jnp.zeros_like(q), q)
    row = lax.broadcasted_iota(jnp.int32, (2 * TK_SB, TK_SB), 0)
    col = lax.broadcasted_iota(jnp.int32, (2 * TK_SB, TK_SB), 1)
    below_diag = col < jnp.where(row >= TK_SB, row - TK_SB, row)

    def visit(qs, kb, run, acc, diag):
        okf = jnp.where(kb >= 0, 1.0, 0.0).astype(F32)
        k0 = pl.multiple_of(jnp.maximum(kb, 0) * TK_SB, TK_SB)
        kblk = k_ref[pl.ds(k0, TK_SB), :]
        vblk = v_ref[pl.ds(k0, TK_SB), :]
        z = _dot_nt(qs, kblk)
        sp = jnp.maximum(z, 0.0) + jnp.log(1.0 + jnp.exp(-jnp.abs(z)))
        lm = jnp.where(below_diag, -sp, 0.0) if diag else -sp * okf
        lm_hi = lm.astype(BF16)
        lm_lo = (lm - lm_hi.astype(F32)).astype(BF16)
        cs = _dot(jnp.concatenate([lm_hi, lm_lo], axis=0), tri_ref[...])
        cs = cs[:2 * TK_SB] + cs[2 * TK_SB:]
        w = jnp.exp(z - sp + cs[:, :TK_SB] + run)
        w = jnp.where(below_diag, w, 0.0) if diag else w * okf
        return run + cs[:, TK_SB:], acc + _dot(w.astype(BF16), vblk)

    def stacked_q(sub):
        rows = slice(sub * TK_SB, (sub + 1) * TK_SB)
        return jnp.concatenate([q_h0[rows], q_h1[rows]], axis=0)

    live = False
    for sub in range(n_sub):
        qs = stacked_q(sub)
        run = jnp.zeros((2 * TK_SB, TK_SB), F32)
        acc = jnp.zeros((2 * TK_SB, LANES), F32)
        for o in range(SB_STATIC_BLOCKS):
            run, acc = visit(qs, qi * n_sub + sub - o, run, acc, o == 0)
        run_ref[sub] = run
        acc_ref[sub] = acc
        live = jnp.logical_or(live, jnp.max(run) > SB_LOG_CUTOFF)

    def cond(c):
        o, live = c
        return jnp.logical_and(live, o <= qi * n_sub + n_sub - 1)

    def body(c):
        o, _ = c
        live = False
        for sub in range(n_sub):
            run, acc = visit(stacked_q(sub), qi * n_sub + sub - o, run_ref[sub], acc_ref[sub], False)
            run_ref[sub] = run
            acc_ref[sub] = acc
            live = jnp.logical_or(live, jnp.max(run) > SB_LOG_CUTOFF)
        return o + 1, live

    lax.while_loop(cond, body, (SB_STATIC_BLOCKS, live))
    for sub in range(n_sub):
        acc = acc_ref[sub]
        rows = slice(sub * TK_SB, (sub + 1) * TK_SB)
        o_ref[rows, :] = jnp.where(lane_sub < SB_HEAD_DIM, acc[:TK_SB], acc[TK_SB:]).astype(o_ref.dtype)


def _sb_attention(q, k, v, batch):
    n, dw = q.shape
    seq = n // batch
    tq = min(TQ_SB, seq)
    nq = seq // tq
    npair = dw // LANES
    s_idx = jnp.arange(TK_SB)[:, None]
    j_idx = jnp.arange(TK_SB)[None, :]
    tri = jnp.concatenate([(s_idx > j_idx), jnp.ones((TK_SB, TK_SB), bool)], axis=1).astype(BF16)
    return pl.pallas_call(
        _sb_attn_kernel,
        grid=(batch, npair, nq),
        in_specs=[pl.BlockSpec((tq, LANES), lambda b, h, i: (b * nq + i, h)),
                  pl.BlockSpec((seq, LANES), lambda b, h, i: (b, h)),
                  pl.BlockSpec((seq, LANES), lambda b, h, i: (b, h)),
                  pl.BlockSpec(tri.shape, lambda b, h, i: (0, 0))],
        out_specs=pl.BlockSpec((tq, LANES), lambda b, h, i: (b * nq + i, h)),
        out_shape=jax.ShapeDtypeStruct((n, dw), BF16),
        scratch_shapes=[pltpu.VMEM((tq // TK_SB, 2 * TK_SB, LANES), F32),
                        pltpu.VMEM((tq // TK_SB, 2 * TK_SB, TK_SB), F32)],
        compiler_params=_cparams(("parallel", "parallel", "arbitrary")),
        name="sb_attn",
    )(q, k, v, tri)


def _sort_pairs(n):
    pairs = []
    p = 1
    while p < n:
        k = p
        while k >= 1:
            for j in range(k % p, n - k, 2 * k):
                for i in range(min(k, n - j - k)):
                    if (i + j) // (2 * p) == (i + j + k) // (2 * p):
                        pairs.append((i + j, i + j + k))
            k //= 2
        p *= 2
    return pairs


def _sublane_all(x, op):
    for shift in (4, 2, 1):
        x = op(x, pltpu.roll(x, shift, 0))
    return x


def _top16(tiles):
    t = list(tiles)
    for i, j in _sort_pairs(len(t)):
        t[i], t[j] = jnp.maximum(t[i], t[j]), jnp.minimum(t[i], t[j])
    n = len(t)
    for shift in (4, 2, 1):
        other = [pltpu.roll(x, shift, 0) for x in t]
        t = [jnp.maximum(t[i], other[n - 1 - i]) for i in range(n)]
        d = n // 2
        while d >= 1:
            for i in range(n):
                if not i & d:
                    t[i], t[i + d] = jnp.maximum(t[i], t[i + d]), jnp.minimum(t[i], t[i + d])
            d //= 2
    return t


def _next_below(tiles, bound):
    m = None
    for x in tiles:
        y = jnp.where(x < bound, x, NEG_INF)
        m = y if m is None else jnp.maximum(m, y)
    return _sublane_all(m, jnp.maximum)


def _rows(tiles):
    sub = lax.broadcasted_iota(jnp.int32, tiles[0].shape, 0)
    out = tiles[-1]
    for r in range(len(tiles) - 2, -1, -1):
        out = jnp.where(sub == r, tiles[r], out)
    return out


def _peer_select_kernel(x_ref, ya_ref, yb_ref, wo_ref, g_ref, wq_ref, k1_ref, k2_ref,
                        h_ref, hx_ref, th_ref, e1_ref, s2_ref, e2_ref, s1_ref, s2s_ref, q_ref):
    t_len = x_ref.shape[0]
    wa = ya_ref.shape[1]
    keys = s1_ref.shape[0]

    @pl.when(pl.program_id(1) == 0)
    def _():
        mix = _dot(ya_ref[...], wo_ref[0:wa, :]) + _dot(yb_ref[...], wo_ref[wa:, :])
        h = x_ref[...] + mix
        h_ref[...] = h
        hx = _rms(h, g_ref[...]).astype(BF16)
        hx_ref[...] = hx
        q_ref[...] = _dot(hx, wq_ref[...]).astype(BF16)

    half = k1_ref.shape[2]
    q0 = pl.multiple_of(pl.program_id(1) * (2 * half), 2 * half)
    s1_ref[...] = _dot_nt(k1_ref[0], q_ref[:, pl.ds(q0, half)])
    s2s_ref[...] = _dot_nt(k2_ref[0], q_ref[:, pl.ds(q0 + half, half)])

    i8 = lax.broadcasted_iota(jnp.int32, (8, LANES), 0)
    neg = jnp.full((8, LANES), NEG_INF, F32)

    def chunk(c, carry):
        l0 = pl.multiple_of(c * LANES, LANES)
        s1 = [s1_ref[r:r + 8, pl.ds(l0, LANES)] for r in range(0, keys, 8)]
        s2 = [s2s_ref[r:r + 8, pl.ds(l0, LANES)] for r in range(0, keys, 8)]
        v1 = _top16(s1)
        v2 = _top16(s2)
        v1_17 = _next_below(s1, v1[PEER_TOPK - 1])
        v2_17 = _next_below(s2, v2[PEER_TOPK - 1])
        v2_lo = _rows(v2[0:8])
        v2_hi = _rows(v2[8:16])
        cands = [v1[0] + v2_lo, v1[0] + v2_hi, v1[1] + v2_lo]
        for a, nb in ((2, 5), (3, 4), (4, 3), (5, 2), (6, 2), (7, 2)):
            cands.append(jnp.where(i8 < nb, v1[a] + v2_lo, NEG_INF))
        cands.append(_rows(v1[8:16]) + v2[0])
        cands.append(jnp.where(i8 == 0, v1[0] + v2_17, jnp.where(i8 == 1, v1_17 + v2[0], NEG_INF)))
        top = _top16(cands + [neg] * (16 - len(cands)))
        t16 = top[PEER_TOPK - 1]
        t17 = _next_below(cands, t16)
        tau = jnp.where(t17 == NEG_INF, t16, 0.5 * (t16 + t17))
        m1 = v1[0]
        m2 = v2[0]
        zs = None
        for x in cands:
            y = jnp.where(x >= tau, jnp.exp(x - (m1 + m2)), 0.0)
            zs = y if zs is None else zs + y
        scale = 0.5 / _sublane_all(zs, jnp.add)
        for i in range(keys // 8):
            rows = slice(8 * i, 8 * i + 8)
            th_ref[0, c, rows, :] = tau - s1[i]
            e1_ref[0, c, rows, :] = jnp.exp(s1[i] - m1)
            s2_ref[0, c, rows, :] = s2[i]
            e2_ref[0, c, rows, :] = jnp.exp(s2[i] - m2) * scale
        return carry

    lax.fori_loop(0, t_len // LANES, chunk, 0, unroll=2)


def _peer_select(x, ya, yb, w_out, g, wq, k1, k2):
    n, d = x.shape
    wa = ya.shape[1]
    heads, keys, half = k1.shape
    assert keys == PEER_KEYS and PEER_TOPK == 16
    t = min(T_PEER, n)
    sel_shape = jax.ShapeDtypeStruct((heads, n // LANES, keys, LANES), F32)
    sel_spec = pl.BlockSpec((1, t // LANES, keys, LANES), lambda i, h: (h, i, 0, 0))
    return pl.pallas_call(
        _peer_select_kernel,
        grid=(n // t, heads),
        in_specs=[pl.BlockSpec((t, d), lambda i, h: (i, 0)),
                  pl.BlockSpec((t, wa), lambda i, h: (i, 0)),
                  pl.BlockSpec((t, yb.shape[1]), lambda i, h: (i, 0)),
                  pl.BlockSpec(w_out.shape, lambda i, h: (0, 0)),
                  pl.BlockSpec((1, d), lambda i, h: (0, 0)),
                  pl.BlockSpec(wq.shape, lambda i, h: (0, 0)),
                  pl.BlockSpec((1, keys, half), lambda i, h: (h, 0, 0)),
                  pl.BlockSpec((1, keys, half), lambda i, h: (h, 0, 0))],
        out_specs=[pl.BlockSpec((t, d), lambda i, h: (i, 0)),
                   pl.BlockSpec((t, d), lambda i, h: (i, 0)),
                   sel_spec, sel_spec, sel_spec, sel_spec],
        out_shape=[jax.ShapeDtypeStruct((n, d), F32), jax.ShapeDtypeStruct((n, d), BF16),
                   sel_shape, sel_shape, sel_shape, sel_shape],
        scratch_shapes=[pltpu.VMEM((keys, t), F32), pltpu.VMEM((keys, t), F32),
                        pltpu.VMEM((t, wq.shape[1]), BF16)],
        compiler_params=_cparams(("parallel", "arbitrary")),
        name="peer_select",
    )(x, ya, yb, w_out.astype(BF16), g.reshape(1, d).astype(F32), wq.astype(BF16),
      k1.astype(BF16), k2.astype(BF16))


def _peer_dense_kernel(hx_ref, th_ref, e1_ref, s2_ref, e2_ref, u_ref, vt_ref, res_ref, gf_ref,
                       out_ref, acc_ref, act0_ref, act1_ref, w0_ref, w1_ref, *, final_norm):
    j = pl.program_id(1)
    n_steps = pl.num_programs(1)
    heads, n_lt, keys, _ = th_ref.shape
    t_len = n_lt * LANES
    e_sub = act0_ref.shape[0]
    n_i1 = e_sub // keys
    d_rows = acc_ref.shape[0] // n_i1
    m_rows = 2 * e_sub // n_i1
    tw = t_len // 2
    rc = 32
    c0 = math.sqrt(2.0 / math.pi)
    c1 = c0 * 0.044715

    never = lax.broadcasted_iota(jnp.int32, (rc, LANES), 0) < jnp.minimum(j, 0)

    def slot_loop(sub, do1, do2, do3):
        act_w, act_r = (act1_ref, act0_ref) if sub else (act0_ref, act1_ref)
        w_w, w_r = (w0_ref, w1_ref) if sub else (w1_ref, w0_ref)
        i1_base = (2 * j + sub - 1) * n_i1

        def body(k, carry):
            deps = []
            if do3:
                d0 = pl.multiple_of(k * d_rows, d_rows)
                p3 = _dot(vt_ref[pl.ds(d0, d_rows), sub * e_sub:(sub + 1) * e_sub], w_r[...])
                acc_ref[pl.ds(d0, d_rows), :] += p3
                deps.append([p3[c * (d_rows // n_lt):c * (d_rows // n_lt) + rc, 0:LANES] for c in range(n_lt)])
            if do1:
                m0 = pl.multiple_of((k // 2) * m_rows, m_rows)
                t0 = pl.multiple_of((k % 2) * tw, tw)
                p1 = _dot_nt(u_ref[pl.ds(sub * e_sub + m0, m_rows), :], hx_ref[pl.ds(t0, tw), :])
                act_w[pl.ds(m0, m_rows), pl.ds(t0, tw)] = p1
                deps.append([p1[c * (m_rows // n_lt):c * (m_rows // n_lt) + rc, 0:LANES] for c in range(n_lt)])
            if do2:
                i1 = i1_base + k
                r0 = pl.multiple_of(k * keys, keys)
                for lt in range(n_lt):
                    ls = slice(lt * LANES, (lt + 1) * LANES)
                    th = [jnp.broadcast_to(th_ref[h, lt, pl.ds(i1, 1), :], (rc, LANES)) for h in range(heads)]
                    e1 = [jnp.broadcast_to(e1_ref[h, lt, pl.ds(i1, 1), :], (rc, LANES)) for h in range(heads)]
                    for r in range(0, keys, rc):
                        g = jnp.zeros((rc, LANES), F32)
                        for h in range(heads):
                            g = g + jnp.where(s2_ref[h, lt, r:r + rc, :] >= th[h], e2_ref[h, lt, r:r + rc, :], 0.0) * e1[h]
                        a = act_r[pl.ds(r0 + r, rc), ls]
                        tanh = jnp.tanh(a * (c0 + c1 * (a * a)))
                        ga = g * a
                        if r == keys - rc:
                            for dep in deps:
                                ga = jnp.where(never, dep[lt], ga)
                        w_w[pl.ds(r0 + r, rc), ls] = (ga + ga * tanh).astype(BF16)
            return carry

        lax.fori_loop(0, n_i1, body, 0, unroll=2)

    @pl.when(j == 0)
    def _():
        acc_ref[...] = jnp.zeros_like(acc_ref)
        slot_loop(0, True, False, False)
        slot_loop(1, True, True, False)

    @pl.when(jnp.logical_and(j > 0, j < n_steps - 1))
    def _():
        slot_loop(0, True, True, True)
        slot_loop(1, True, True, True)

    @pl.when(j == n_steps - 1)
    def _():
        slot_loop(0, False, True, True)
        slot_loop(1, False, False, True)
        o = res_ref[...] + acc_ref[...].T
        if final_norm:
            o = _rms(o, gf_ref[...])
        out_ref[...] = o


def _peer_dense(hx, th, e1, s2, e2, u_tab, vt_tab, res, g_final, final_norm):
    n, d = res.shape
    heads, _, keys, _ = th.shape
    n_exp = u_tab.shape[0]
    t = min(T_PEER, n)
    e_blk = min(E_BLK, n_exp)
    nj = n_exp // e_blk
    once = pl.Buffered(1)
    sel_spec = pl.BlockSpec((heads, t // LANES, keys, LANES), lambda i, j: (0, i, 0, 0), pipeline_mode=once)
    return pl.pallas_call(
        functools.partial(_peer_dense_kernel, final_norm=final_norm),
        grid=(n // t, nj + 1),
        in_specs=[pl.BlockSpec((t, d), lambda i, j: (i, 0), pipeline_mode=once),
                  sel_spec, sel_spec, sel_spec, sel_spec,
                  pl.BlockSpec((e_blk, d), lambda i, j: (jnp.minimum(j, nj - 1), 0)),
                  pl.BlockSpec((d, e_blk), lambda i, j: (0, jnp.maximum(j - 1, 0))),
                  pl.BlockSpec((t, d), lambda i, j: (i, 0), pipeline_mode=once),
                  pl.BlockSpec((1, d), lambda i, j: (0, 0))],
        out_specs=pl.BlockSpec((t, d), lambda i, j: (i, 0)),
        out_shape=jax.ShapeDtypeStruct((n, d), F32),
        scratch_shapes=[pltpu.VMEM((d, t), F32),
                        pltpu.VMEM((e_blk // 2, t), F32), pltpu.VMEM((e_blk // 2, t), F32),
                        pltpu.VMEM((e_blk // 2, t), BF16), pltpu.VMEM((e_blk // 2, t), BF16)],
        compiler_params=_cparams(("parallel", "arbitrary")),
        name="peer_dense",
    )(hx, th, e1, s2, e2, u_tab, vt_tab, res, g_final.reshape(1, d).astype(F32))


def _peer_block(x, ya, yb, w_out, g_ffn, wq, k1, k2, u_tab, v_tab, g_final, final_norm):
    h, hx, th, e1, s2, e2 = _peer_select(x, ya, yb, w_out, g_ffn, wq, k1, k2)
    return _peer_dense(hx, th, e1, s2, e2, u_tab.astype(BF16), v_tab.astype(BF16).T, h, g_final,
                       final_norm)


def kernel(x, l0_norm_mix_g, l0_w_in, a_lam_re, a_lam_im, a_log_dt, a_b_re, a_b_im, a_c_re, a_c_im, a_d, a_glu_w, a_glu_b, b_conv_w, l0_w_out, l0_norm_ffn_g, l0_peer_wq, l0_peer_k1, l0_peer_k2, l0_peer_u, l0_peer_v, l1_norm_mix_g, l1_w_in, c_vnorm_g, c_ws, c_bs, l1_w_out, l1_norm_ffn_g, l1_peer_wq, l1_peer_k1, l1_peer_k2, l1_peer_u, l1_peer_v, final_norm_g):
    batch, seq, d = x.shape
    n = batch * seq
    h = x.reshape(n, d)

    (p0,) = _norm_matmul(h, l0_norm_mix_g.astype(F32), l0_w_in.astype(BF16),
                         ((0, l0_w_in.shape[1], 1.0),), (F32,))
    bq, are, aim, cq, apr, api = _s5_params(a_lam_re, a_lam_im, a_log_dt, a_b_re, a_b_im, a_c_re, a_c_im,
                                            min(T_S5, seq) // S5_SEGMENTS)
    ya, yb = _even_mixer(p0, batch, bq, are, aim, apr, api, cq, a_d, a_glu_w, a_glu_b, b_conv_w)
    h = _peer_block(h, ya, yb, l0_w_out, l0_norm_ffn_g, l0_peer_wq, l0_peer_k1, l0_peer_k2,
                    l0_peer_u, l0_peer_v, final_norm_g, False)

    cw = c_vnorm_g.shape[0]
    dw = (l1_w_in.shape[1] - 2 * cw) // 3
    o = 2 * cw
    splits = ((0, o, 1.0), (o, o + dw, SB_HEAD_DIM ** -0.5), (o + dw, o + 2 * dw, 1.0),
              (o + 2 * dw, o + 3 * dw, 1.0))
    uv, q, k, v = _norm_matmul(h, l1_norm_mix_g.astype(F32), l1_w_in.astype(BF16), splits,
                               (F32, BF16, BF16, BF16))
    yc = _gmlp(uv, c_vnorm_g, c_ws, c_bs)
    yd = _sb_attention(q, k, v, batch)
    h = _peer_block(h, yc, yd, l1_w_out, l1_norm_ffn_g, l1_peer_wq, l1_peer_k1, l1_peer_k2,
                    l1_peer_u, l1_peer_v, final_norm_g, True)
    return h.reshape(batch, seq, d)
```

```python
import functools
import math

import jax
import jax.numpy as jnp
from jax import lax
from jax.experimental import pallas as pl
from jax.experimental.pallas import tpu as pltpu

F32 = jnp.float32
BF16 = jnp.bfloat16
NEG_INF = float("-inf")

RMS_EPS = 1e-6
LANES = 128
S5_GROUP = 16
S5_QUARTERS = 4
S5_SEGMENTS = 8
S5_TOGETHER = 4
PEER_KEYS = 128
PEER_TOPK = 16
SB_HEAD_DIM = 64
GMLP_CHUNK = 128
GMLP_HEAD_DIM = 64
SB_LOG_CUTOFF = -104.0

TM_PROJ = 1024
T_S5 = 512
T_GMLP = 512
TQ_SB = 512
TK_SB = 128
SB_STATIC_BLOCKS = 3
T_PEER = 1024
E_BLK = 1024
VMEM_LIMIT = 58 * 1024 * 1024


def _cparams(sem):
    return pltpu.CompilerParams(dimension_semantics=sem, vmem_limit_bytes=VMEM_LIMIT)


def _dot(a, b):
    return jnp.dot(a, b, preferred_element_type=F32)


def _dot_nt(a, b):
    return lax.dot_general(a, b, (((1,), (1,)), ((), ())), preferred_element_type=F32)


def _gelu(x):
    c = math.sqrt(2.0 / math.pi)
    return 0.5 * x * (1.0 + jnp.tanh(c * (x + 0.044715 * (x * x * x))))


def _rms(x, g):
    return x * lax.rsqrt(jnp.mean(x * x, axis=-1, keepdims=True) + RMS_EPS) * g


def _norm_matmul_kernel(x_ref, g_ref, w_ref, *out_refs, splits):
    xn = _rms(x_ref[...], g_ref[...]).astype(BF16)
    p = _dot(xn, w_ref[...])
    for o_ref, (lo, hi, scale) in zip(out_refs, splits):
        part = p[:, lo:hi]
        if scale != 1.0:
            part = part * scale
        o_ref[...] = part.astype(o_ref.dtype)


def _norm_matmul(x, g, w, splits, dtypes):
    n, d = x.shape
    m = w.shape[1]
    tm = min(TM_PROJ, n)
    out_shape = [jax.ShapeDtypeStruct((n, hi - lo), dt) for (lo, hi, _), dt in zip(splits, dtypes)]
    out_specs = [pl.BlockSpec((tm, hi - lo), lambda i: (i, 0)) for (lo, hi, _) in splits]
    return pl.pallas_call(
        functools.partial(_norm_matmul_kernel, splits=splits),
        grid=(n // tm,),
        in_specs=[pl.BlockSpec((tm, d), lambda i: (i, 0)),
                  pl.BlockSpec((1, d), lambda i: (0, 0)),
                  pl.BlockSpec((d, m), lambda i: (0, 0))],
        out_specs=out_specs,
        out_shape=out_shape,
        compiler_params=_cparams(("parallel",)),
        name="norm_matmul",
    )(x, g.reshape(1, d), w)


def _s5_pitch(seg):
    tiles = -(-seg // 8)
    return 8 * (tiles if tiles % 2 else tiles + 1)


def _even_mixer_kernel(p_ref, bq_ref, are_ref, aim_ref, apr_ref, api_ref, cq_ref, d_ref, gw_ref, gb_ref, cw_ref,
                       ya_ref, yb_ref, bus_ref, hst_ref, zs_ref):
    t_len = p_ref.shape[0]
    aw = ya_ref.shape[1]
    n_pair = aw // LANES
    seg = t_len // S5_SEGMENTS
    pitch = _s5_pitch(seg)

    @pl.when(pl.program_id(1) == 0)
    def _():
        hst_ref[...] = jnp.zeros_like(hst_ref)
        zs_ref[0:8, :] = jnp.zeros((8, zs_ref.shape[1]), F32)

    u = p_ref[:, 0:aw]
    ub = u.astype(BF16)
    sub = lax.broadcasted_iota(jnp.int32, (S5_SEGMENTS, LANES), 0)
    ys = []
    zero = jnp.zeros((S5_SEGMENTS, LANES), F32)
    rows_at = lambda t: pl.ds(t, S5_SEGMENTS, stride=pitch)
    for q_lo in range(0, S5_QUARTERS, S5_TOGETHER):
        quarters = range(q_lo, q_lo + S5_TOGETHER)
        slabs = [(q, (q - q_lo) * 2 * n_pair + j, (q - q_lo) * 2 * n_pair + n_pair + j)
                 for q in quarters for j in range(n_pair)]
        for q in quarters:
            bu = _dot(ub[:, q * LANES:(q + 1) * LANES], bq_ref[q])
            for j in range(2 * n_pair):
                for sg in range(S5_SEGMENTS):
                    bus_ref[(q - q_lo) * 2 * n_pair + j, sg * pitch:sg * pitch + seg, :] = (
                        bu[sg * seg:(sg + 1) * seg, j * LANES:(j + 1) * LANES])

        def a_rows(ref, q, j):
            return ref[q:q + 1, (j % n_pair) * LANES:(j % n_pair + 1) * LANES]

        def scan_step(store):
            def step(t, carry, slabs=slabs):
                out = []
                for i, (q, s_re, s_im) in enumerate(slabs):
                    hr, hi = carry[2 * i], carry[2 * i + 1]
                    ar = a_rows(are_ref, q, i)
                    ai = a_rows(aim_ref, q, i)
                    nr = ar * hr - ai * hi + bus_ref[s_re, rows_at(t), :]
                    ni = ar * hi + ai * hr + bus_ref[s_im, rows_at(t), :]
                    if store:
                        bus_ref[s_re, rows_at(t), :] = nr
                        bus_ref[s_im, rows_at(t), :] = ni
                    out += [nr, ni]
                return tuple(out)
            return step

        ends = lax.fori_loop(0, seg, scan_step(False), (zero,) * (2 * len(slabs)), unroll=2)

        h_in = []
        for i, (q, s_re, s_im) in enumerate(slabs):
            j = i % n_pair
            pr = apr_ref[q * n_pair + j, :, :]
            pi = api_ref[q * n_pair + j, :, :]
            cr = hst_ref[2 * q:2 * q + 1, j * LANES:(j + 1) * LANES]
            ci = hst_ref[2 * q + 1:2 * q + 2, j * LANES:(j + 1) * LANES]
            hr_in, hi_in = zero, zero
            for sg in range(S5_SEGMENTS):
                hr_in = jnp.where(sub == sg, cr, hr_in)
                hi_in = jnp.where(sub == sg, ci, hi_in)
                er = ends[2 * i][sg:sg + 1, :]
                ei = ends[2 * i + 1][sg:sg + 1, :]
                cr, ci = pr * cr - pi * ci + er, pr * ci + pi * cr + ei
            hst_ref[2 * q:2 * q + 1, j * LANES:(j + 1) * LANES] = cr
            hst_ref[2 * q + 1:2 * q + 2, j * LANES:(j + 1) * LANES] = ci
            h_in += [hr_in, hi_in]

        lax.fori_loop(0, seg, scan_step(True), tuple(h_in), unroll=2)

        for q in quarters:
            cols = []
            for j in range(2 * n_pair):
                cols.append(jnp.concatenate(
                    [bus_ref[(q - q_lo) * 2 * n_pair + j, sg * pitch:sg * pitch + seg, :]
                     for sg in range(S5_SEGMENTS)], axis=0))
            ys.append(_dot(jnp.concatenate(cols, axis=1).astype(BF16), cq_ref[q]))

    y = jnp.concatenate(ys, axis=1) + d_ref[...] * u
    y = _gelu(y)
    gate = jax.nn.sigmoid(_dot(y.astype(BF16), gw_ref[...]) + gb_ref[...])
    ya_ref[...] = (y * gate).astype(ya_ref.dtype)

    bw = yb_ref.shape[1]
    z = p_ref[:, aw + bw:aw + 2 * bw] * p_ref[:, aw + 2 * bw:aw + 3 * bw]
    zs_ref[8:8 + t_len, :] = z
    conv = (cw_ref[2:3, :] * z + cw_ref[1:2, :] * zs_ref[7:7 + t_len, :]
            + cw_ref[0:1, :] * zs_ref[6:6 + t_len, :])
    yb_ref[...] = (p_ref[:, aw:aw + bw] * conv).astype(yb_ref.dtype)
    zs_ref[0:8, :] = zs_ref[t_len:t_len + 8, :]


def _s5_params(lam_re, lam_im, log_dt, b_re, b_im, c_re, c_im, n_pow):
    g, p = lam_re.shape
    gq = g // S5_QUARTERS
    dt = jnp.exp(log_dt.astype(F32))[:, None]
    lr = lam_re.astype(F32)
    li = lam_im.astype(F32)
    mag = jnp.exp(lr * dt)
    ar = mag * jnp.cos(li * dt)
    ai = mag * jnp.sin(li * dt)
    den = lr * lr + li * li
    nr = ar - 1.0
    fr = (nr * lr + ai * li) / den
    fi = (ai * lr - nr * li) / den
    bre = b_re.astype(F32)
    bim = b_im.astype(F32)
    bbar_re = fr[..., None] * bre - fi[..., None] * bim
    bbar_im = fr[..., None] * bim + fi[..., None] * bre
    eye = jnp.eye(gq, dtype=F32)

    def in_blockdiag(m):
        m = m.reshape(S5_QUARTERS, gq, p, S5_GROUP)
        return jnp.einsum('qgpc,gh->qgchp', m, eye).reshape(S5_QUARTERS, gq * S5_GROUP, gq * p)

    def out_blockdiag(m):
        m = m.reshape(S5_QUARTERS, gq, S5_GROUP, p)
        return jnp.einsum('qgcp,gh->qgphc', m, eye).reshape(S5_QUARTERS, gq * p, gq * S5_GROUP)

    bq = jnp.concatenate([in_blockdiag(bbar_re), in_blockdiag(bbar_im)], axis=2).astype(BF16)
    cq = jnp.concatenate([out_blockdiag(c_re.astype(F32)), -out_blockdiag(c_im.astype(F32))],
                         axis=1).astype(BF16)
    are = ar.reshape(S5_QUARTERS, gq * p)
    aim = ai.reshape(S5_QUARTERS, gq * p)

    def next_power(_, c):
        cr, ci = c
        return cr * ar - ci * ai, cr * ai + ci * ar

    pw_re, pw_im = lax.fori_loop(1, n_pow, next_power, (ar, ai))
    slabs = lambda x: x.reshape(g * p // LANES, 1, LANES)
    return bq, are, aim, cq, slabs(pw_re), slabs(pw_im)


def _even_mixer(p, batch, bq, are, aim, apr, api, cq, d_skip, glu_w, glu_b, conv_w):
    n, width = p.shape
    seq = n // batch
    aw = d_skip.size
    bw = conv_w.shape[1]
    t = min(T_S5, seq)
    nt = seq // t
    qs = 2 * aw
    full = lambda shape: pl.BlockSpec(shape, lambda b, i: (0,) * len(shape))
    return pl.pallas_call(
        _even_mixer_kernel,
        grid=(batch, nt),
        in_specs=[pl.BlockSpec((t, width), lambda b, i: (b * nt + i, 0)),
                  full(bq.shape), full(are.shape), full(aim.shape), full(apr.shape), full(api.shape), full(cq.shape),
                  full((1, aw)), full(glu_w.shape), full((1, aw)), full(conv_w.shape)],
        out_specs=[pl.BlockSpec((t, aw), lambda b, i: (b * nt + i, 0)),
                   pl.BlockSpec((t, bw), lambda b, i: (b * nt + i, 0))],
        out_shape=[jax.ShapeDtypeStruct((n, aw), BF16), jax.ShapeDtypeStruct((n, bw), BF16)],
        scratch_shapes=[pltpu.VMEM((S5_TOGETHER * qs // LANES, S5_SEGMENTS * _s5_pitch(t // S5_SEGMENTS), LANES), F32),
                        pltpu.VMEM((2 * S5_QUARTERS, aw), F32),
                        pltpu.VMEM((t + 8, bw), F32)],
        compiler_params=_cparams(("arbitrary", "arbitrary")),
        name="even_mixer",
    )(p, bq, are, aim, apr, api, cq, d_skip.reshape(1, aw).astype(F32), glu_w.astype(BF16),
      glu_b.reshape(1, aw).astype(F32), conv_w.astype(F32))


def _gmlp_kernel(uv_ref, g_ref, wm_ref, bias_ref, y_ref):
    cw = y_ref.shape[1]
    t_len = y_ref.shape[0]
    u = _gelu(uv_ref[:, 0:cw])
    v = _rms(_gelu(uv_ref[:, cw:2 * cw]), g_ref[...]).astype(BF16)
    lane = lax.broadcasted_iota(jnp.int32, (GMLP_CHUNK, LANES), 1)
    first_head = lane < GMLP_HEAD_DIM
    for c in range(t_len // GMLP_CHUNK):
        r0 = c * GMLP_CHUNK
        tiles = []
        for j in range(cw // LANES):
            vc = v[r0:r0 + GMLP_CHUNK, j * LANES:(j + 1) * LANES]
            tiles.append(jnp.where(first_head, _dot(wm_ref[2 * j], vc), _dot(wm_ref[2 * j + 1], vc)))
        y = jnp.concatenate(tiles, axis=1) + bias_ref[...]
        y_ref[r0:r0 + GMLP_CHUNK, :] = (u[r0:r0 + GMLP_CHUNK, :] * y).astype(y_ref.dtype)


def _gmlp(uv, vnorm_g, ws, bs):
    n = uv.shape[0]
    cw = uv.shape[1] // 2
    heads = ws.shape[0]
    t = min(T_GMLP, n)
    tril = jnp.tril(jnp.ones((GMLP_CHUNK, GMLP_CHUNK), dtype=bool))
    wm = jnp.where(tril, ws, jnp.zeros_like(ws)).astype(BF16)
    bias = jnp.repeat(bs.T.astype(F32), cw // heads, axis=1)
    return pl.pallas_call(
        _gmlp_kernel,
        grid=(n // t,),
        in_specs=[pl.BlockSpec((t, 2 * cw), lambda i: (i, 0)),
                  pl.BlockSpec((1, cw), lambda i: (0, 0)),
                  pl.BlockSpec(wm.shape, lambda i: (0, 0, 0)),
                  pl.BlockSpec(bias.shape, lambda i: (0, 0))],
        out_specs=pl.BlockSpec((t, cw), lambda i: (i, 0)),
        out_shape=jax.ShapeDtypeStruct((n, cw), BF16),
        compiler_params=_cparams(("parallel",)),
        name="gmlp",
    )(uv, vnorm_g.reshape(1, cw).astype(F32), wm, bias)


def _sb_attn_kernel(q_ref, k_ref, v_ref, tri_ref, o_ref, acc_ref, run_ref):
    qi = pl.program_id(2)
    n_sub = q_ref.shape[0] // TK_SB
    q = q_ref[...]
    lane = lax.broadcasted_iota(jnp.int32, q.shape, 1)
    lane_sub = lax.broadcasted_iota(jnp.int32, (TK_SB, LANES), 1)
    q_h0 = jnp.where(lane < SB_HEAD_DIM, q, jnp.zeros_like(q))
    q_h1 = jnp.where(lane < SB_HEAD_DIM, jnp.zeros_like(q), q)
    row = lax.broadcasted_iota(jnp.int32, (2 * TK_SB, TK_SB), 0)
    col = lax.broadcasted_iota(jnp.int32, (2 * TK_SB, TK_SB), 1)
    below_diag = col < jnp.where(row >= TK_SB, row - TK_SB, row)

    def visit(qs, kb, run, acc, diag):
        okf = jnp.where(kb >= 0, 1.0, 0.0).astype(F32)
        k0 = pl.multiple_of(jnp.maximum(kb, 0) * TK_SB, TK_SB)
        kblk = k_ref[pl.ds(k0, TK_SB), :]
        vblk = v_ref[pl.ds(k0, TK_SB), :]
        z = _dot_nt(qs, kblk)
        sp = jnp.maximum(z, 0.0) + jnp.log(1.0 + jnp.exp(-jnp.abs(z)))
        lm = jnp.where(below_diag, -sp, 0.0) if diag else -sp * okf
        lm_hi = lm.astype(BF16)
        lm_lo = (lm - lm_hi.astype(F32)).astype(BF16)
        cs = _dot(jnp.concatenate([lm_hi, lm_lo], axis=0), tri_ref[...])
        cs = cs[:2 * TK_SB] + cs[2 * TK_SB:]
        w = jnp.exp(z - sp + cs[:, :TK_SB] + run)
        w = jnp.where(below_diag, w, 0.0) if diag else w * okf
        return run + cs[:, TK_SB:], acc + _dot(w.astype(BF16), vblk)

    def stacked_q(sub):
        rows = slice(sub * TK_SB, (sub + 1) * TK_SB)
        return jnp.concatenate([q_h0[rows], q_h1[rows]], axis=0)

    live = False
    for sub in range(n_sub):
        qs = stacked_q(sub)
        run = jnp.zeros((2 * TK_SB, TK_SB), F32)
        acc = jnp.zeros((2 * TK_SB, LANES), F32)
        for o in range(SB_STATIC_BLOCKS):
            run, acc = visit(qs, qi * n_sub + sub - o, run, acc, o == 0)
        run_ref[sub] = run
        acc_ref[sub] = acc
        live = jnp.logical_or(live, jnp.max(run) > SB_LOG_CUTOFF)

    def cond(c):
        o, live = c
        return jnp.logical_and(live, o <= qi * n_sub + n_sub - 1)

    def body(c):
        o, _ = c
        live = False
        for sub in range(n_sub):
            run, acc = visit(stacked_q(sub), qi * n_sub + sub - o, run_ref[sub], acc_ref[sub], False)
            run_ref[sub] = run
            acc_ref[sub] = acc
            live = jnp.logical_or(live, jnp.max(run) > SB_LOG_CUTOFF)
        return o + 1, live

    lax.while_loop(cond, body, (SB_STATIC_BLOCKS, live))
    for sub in range(n_sub):
        acc = acc_ref[sub]
        rows = slice(sub * TK_SB, (sub + 1) * TK_SB)
        o_ref[rows, :] = jnp.where(lane_sub < SB_HEAD_DIM, acc[:TK_SB], acc[TK_SB:]).astype(o_ref.dtype)


def _sb_attention(q, k, v, batch):
    n, dw = q.shape
    seq = n // batch
    tq = min(TQ_SB, seq)
    nq = seq // tq
    npair = dw // LANES
    s_idx = jnp.arange(TK_SB)[:, None]
    j_idx = jnp.arange(TK_SB)[None, :]
    tri = jnp.concatenate([(s_idx > j_idx), jnp.ones((TK_SB, TK_SB), bool)], axis=1).astype(BF16)
    return pl.pallas_call(
        _sb_attn_kernel,
        grid=(batch, npair, nq),
        in_specs=[pl.BlockSpec((tq, LANES), lambda b, h, i: (b * nq + i, h)),
                  pl.BlockSpec((seq, LANES), lambda b, h, i: (b, h)),
                  pl.BlockSpec((seq, LANES), lambda b, h, i: (b, h)),
                  pl.BlockSpec(tri.shape, lambda b, h, i: (0, 0))],
        out_specs=pl.BlockSpec((tq, LANES), lambda b, h, i: (b * nq + i, h)),
        out_shape=jax.ShapeDtypeStruct((n, dw), BF16),
        scratch_shapes=[pltpu.VMEM((tq // TK_SB, 2 * TK_SB, LANES), F32),
                        pltpu.VMEM((tq // TK_SB, 2 * TK_SB, TK_SB), F32)],
        compiler_params=_cparams(("parallel", "parallel", "arbitrary")),
        name="sb_attn",
    )(q, k, v, tri)


def _sort_pairs(n):
    pairs = []
    p = 1
    while p < n:
        k = p
        while k >= 1:
            for j in range(k % p, n - k, 2 * k):
                for i in range(min(k, n - j - k)):
                    if (i + j) // (2 * p) == (i + j + k) // (2 * p):
                        pairs.append((i + j, i + j + k))
            k //= 2
        p *= 2
    return pairs


def _sublane_all(x, op):
    for shift in (4, 2, 1):
        x = op(x, pltpu.roll(x, shift, 0))
    return x


def _top16(tiles):
    t = list(tiles)
    for i, j in _sort_pairs(len(t)):
        t[i], t[j] = jnp.maximum(t[i], t[j]), jnp.minimum(t[i], t[j])
    n = len(t)
    for shift in (4, 2, 1):
        other = [pltpu.roll(x, shift, 0) for x in t]
        t = [jnp.maximum(t[i], other[n - 1 - i]) for i in range(n)]
        d = n // 2
        while d >= 1:
            for i in range(n):
                if not i & d:
                    t[i], t[i + d] = jnp.maximum(t[i], t[i + d]), jnp.minimum(t[i], t[i + d])
            d //= 2
    return t


def _next_below(tiles, bound):
    m = None
    for x in tiles:
        y = jnp.where(x < bound, x, NEG_INF)
        m = y if m is None else jnp.maximum(m, y)
    return _sublane_all(m, jnp.maximum)


def _rows(tiles):
    sub = lax.broadcasted_iota(jnp.int32, tiles[0].shape, 0)
    out = tiles[-1]
    for r in range(len(tiles) - 2, -1, -1):
        out = jnp.where(sub == r, tiles[r], out)
    return out


def _peer_select_kernel(x_ref, ya_ref, yb_ref, wo_ref, g_ref, wq_ref, k1_ref, k2_ref,
                        h_ref, hx_ref, th_ref, e1_ref, s2_ref, e2_ref, s1_ref, s2s_ref, q_ref):
    t_len = x_ref.shape[0]
    wa = ya_ref.shape[1]
    keys = s1_ref.shape[0]

    @pl.when(pl.program_id(1) == 0)
    def _():
        mix = _dot(ya_ref[...], wo_ref[0:wa, :]) + _dot(yb_ref[...], wo_ref[wa:, :])
        h = x_ref[...] + mix
        h_ref[...] = h
        hx = _rms(h, g_ref[...]).astype(BF16)
        hx_ref[...] = hx
        q_ref[...] = _dot(hx, wq_ref[...]).astype(BF16)

    half = k1_ref.shape[2]
    q0 = pl.multiple_of(pl.program_id(1) * (2 * half), 2 * half)
    s1_ref[...] = _dot_nt(k1_ref[0], q_ref[:, pl.ds(q0, half)])
    s2s_ref[...] = _dot_nt(k2_ref[0], q_ref[:, pl.ds(q0 + half, half)])

    i8 = lax.broadcasted_iota(jnp.int32, (8, LANES), 0)
    neg = jnp.full((8, LANES), NEG_INF, F32)

    def chunk(c, carry):
        l0 = pl.multiple_of(c * LANES, LANES)
        s1 = [s1_ref[r:r + 8, pl.ds(l0, LANES)] for r in range(0, keys, 8)]
        s2 = [s2s_ref[r:r + 8, pl.ds(l0, LANES)] for r in range(0, keys, 8)]
        v1 = _top16(s1)
        v2 = _top16(s2)
        v1_17 = _next_below(s1, v1[PEER_TOPK - 1])
        v2_17 = _next_below(s2, v2[PEER_TOPK - 1])
        v2_lo = _rows(v2[0:8])
        v2_hi = _rows(v2[8:16])
        cands = [v1[0] + v2_lo, v1[0] + v2_hi, v1[1] + v2_lo]
        for a, nb in ((2, 5), (3, 4), (4, 3), (5, 2), (6, 2), (7, 2)):
            cands.append(jnp.where(i8 < nb, v1[a] + v2_lo, NEG_INF))
        cands.append(_rows(v1[8:16]) + v2[0])
        cands.append(jnp.where(i8 == 0, v1[0] + v2_17, jnp.where(i8 == 1, v1_17 + v2[0], NEG_INF)))
        top = _top16(cands + [neg] * (16 - len(cands)))
        t16 = top[PEER_TOPK - 1]
        t17 = _next_below(cands, t16)
        tau = jnp.where(t17 == NEG_INF, t16, 0.5 * (t16 + t17))
        m1 = v1[0]
        m2 = v2[0]
        zs = None
        for x in cands:
            y = jnp.where(x >= tau, jnp.exp(x - (m1 + m2)), 0.0)
            zs = y if zs is None else zs + y
        scale = 0.5 / _sublane_all(zs, jnp.add)
        for i in range(keys // 8):
            rows = slice(8 * i, 8 * i + 8)
            th_ref[0, c, rows, :] = tau - s1[i]
            e1_ref[0, c, rows, :] = jnp.exp(s1[i] - m1)
            s2_ref[0, c, rows, :] = s2[i]
            e2_ref[0, c, rows, :] = jnp.exp(s2[i] - m2) * scale
        return carry

    lax.fori_loop(0, t_len // LANES, chunk, 0, unroll=2)


def _peer_select(x, ya, yb, w_out, g, wq, k1, k2):
    n, d = x.shape
    wa = ya.shape[1]
    heads, keys, half = k1.shape
    assert keys == PEER_KEYS and PEER_TOPK == 16
    t = min(T_PEER, n)
    sel_shape = jax.ShapeDtypeStruct((heads, n // LANES, keys, LANES), F32)
    sel_spec = pl.BlockSpec((1, t // LANES, keys, LANES), lambda i, h: (h, i, 0, 0))
    return pl.pallas_call(
        _peer_select_kernel,
        grid=(n // t, heads),
        in_specs=[pl.BlockSpec((t, d), lambda i, h: (i, 0)),
                  pl.BlockSpec((t, wa), lambda i, h: (i, 0)),
                  pl.BlockSpec((t, yb.shape[1]), lambda i, h: (i, 0)),
                  pl.BlockSpec(w_out.shape, lambda i, h: (0, 0)),
                  pl.BlockSpec((1, d), lambda i, h: (0, 0)),
                  pl.BlockSpec(wq.shape, lambda i, h: (0, 0)),
                  pl.BlockSpec((1, keys, half), lambda i, h: (h, 0, 0)),
                  pl.BlockSpec((1, keys, half), lambda i, h: (h, 0, 0))],
        out_specs=[pl.BlockSpec((t, d), lambda i, h: (i, 0)),
                   pl.BlockSpec((t, d), lambda i, h: (i, 0)),
                   sel_spec, sel_spec, sel_spec, sel_spec],
        out_shape=[jax.ShapeDtypeStruct((n, d), F32), jax.ShapeDtypeStruct((n, d), BF16),
                   sel_shape, sel_shape, sel_shape, sel_shape],
        scratch_shapes=[pltpu.VMEM((keys, t), F32), pltpu.VMEM((keys, t), F32),
                        pltpu.VMEM((t, wq.shape[1]), BF16)],
        compiler_params=_cparams(("parallel", "arbitrary")),
        name="peer_select",
    )(x, ya, yb, w_out.astype(BF16), g.reshape(1, d).astype(F32), wq.astype(BF16),
      k1.astype(BF16), k2.astype(BF16))


def _peer_dense_kernel(hx_ref, thp_ref, thc_ref, e1p_ref, e1c_ref, s2_ref, e2_ref, u_ref, vt_ref, res_ref, gf_ref,
                       out_ref, acc_ref, act0_ref, act1_ref, w0_ref, w1_ref, *, final_norm):
    j = pl.program_id(1)
    n_steps = pl.num_programs(1)
    heads, n_lt, keys, _ = s2_ref.shape
    t_len = n_lt * LANES
    e_sub = act0_ref.shape[0]
    n_i1 = e_sub // keys
    d_rows = acc_ref.shape[0] // n_i1
    m_rows = 2 * e_sub // n_i1
    tw = t_len // 2
    rc = 32
    c0 = math.sqrt(2.0 / math.pi)
    c1 = c0 * 0.044715

    never = lax.broadcasted_iota(jnp.int32, (rc, LANES), 0) < jnp.minimum(j, 0)

    def slot_loop(sub, do1, do2, do3):
        act_w, act_r = (act1_ref, act0_ref) if sub else (act0_ref, act1_ref)
        w_w, w_r = (w0_ref, w1_ref) if sub else (w1_ref, w0_ref)
        th_ref, e1_ref = (thc_ref, e1c_ref) if sub else (thp_ref, e1p_ref)
        row_base = 0 if sub else n_i1

        def body(k, carry):
            deps = []
            if do3:
                d0 = pl.multiple_of(k * d_rows, d_rows)
                p3 = _dot(vt_ref[pl.ds(d0, d_rows), sub * e_sub:(sub + 1) * e_sub], w_r[...])
                acc_ref[pl.ds(d0, d_rows), :] += p3
                deps.append([p3[c * (d_rows // n_lt):c * (d_rows // n_lt) + rc, 0:LANES] for c in range(n_lt)])
            if do1:
                m0 = pl.multiple_of((k // 2) * m_rows, m_rows)
                t0 = pl.multiple_of((k % 2) * tw, tw)
                p1 = _dot_nt(u_ref[pl.ds(sub * e_sub + m0, m_rows), :], hx_ref[pl.ds(t0, tw), :])
                act_w[pl.ds(m0, m_rows), pl.ds(t0, tw)] = p1
                deps.append([p1[c * (m_rows // n_lt):c * (m_rows // n_lt) + rc, 0:LANES] for c in range(n_lt)])
            if do2:
                i1 = row_base + k
                r0 = pl.multiple_of(k * keys, keys)
                for lt in range(n_lt):
                    ls = slice(lt * LANES, (lt + 1) * LANES)
                    th = [jnp.broadcast_to(th_ref[h, lt, pl.ds(i1, 1), :], (rc, LANES)) for h in range(heads)]
                    e1 = [jnp.broadcast_to(e1_ref[h, lt, pl.ds(i1, 1), :], (rc, LANES)) for h in range(heads)]
                    for r in range(0, keys, rc):
                        g = jnp.zeros((rc, LANES), F32)
                        for h in range(heads):
                            g = g + jnp.where(s2_ref[h, lt, r:r + rc, :] >= th[h], e2_ref[h, lt, r:r + rc, :], 0.0) * e1[h]
                        a = act_r[pl.ds(r0 + r, rc), ls]
                        tanh = jnp.tanh(a * (c0 + c1 * (a * a)))
                        ga = g * a
                        if r == keys - rc:
                            for dep in deps:
                                ga = jnp.where(never, dep[lt], ga)
                        w_w[pl.ds(r0 + r, rc), ls] = (ga + ga * tanh).astype(BF16)
            return carry

        lax.fori_loop(0, n_i1, body, 0, unroll=2)

    @pl.when(j == 0)
    def _():
        acc_ref[...] = jnp.zeros_like(acc_ref)
        slot_loop(0, True, False, False)
        slot_loop(1, True, True, False)

    @pl.when(jnp.logical_and(j > 0, j < n_steps - 1))
    def _():
        slot_loop(0, True, True, True)
        slot_loop(1, True, True, True)

    @pl.when(j == n_steps - 1)
    def _():
        slot_loop(0, False, True, True)
        slot_loop(1, False, False, True)
        o = res_ref[...] + acc_ref[...].T
        if final_norm:
            o = _rms(o, gf_ref[...])
        out_ref[...] = o


def _peer_dense(hx, th, e1, s2, e2, u_tab, vt_tab, res, g_final, final_norm):
    n, d = res.shape
    heads, _, keys, _ = th.shape
    n_exp = u_tab.shape[0]
    t = min(T_PEER, n)
    e_blk = min(E_BLK, n_exp)
    nj = n_exp // e_blk
    sel_spec = pl.BlockSpec((heads, t // LANES, keys, LANES), lambda i, j: (0, i, 0, 0))
    rows = e_blk // keys
    row_prev = pl.BlockSpec((heads, t // LANES, rows, LANES), lambda i, j: (0, i, jnp.maximum(j - 1, 0), 0))
    row_cur = pl.BlockSpec((heads, t // LANES, rows, LANES), lambda i, j: (0, i, jnp.minimum(j, nj - 1), 0))
    return pl.pallas_call(
        functools.partial(_peer_dense_kernel, final_norm=final_norm),
        grid=(n // t, nj + 1),
        in_specs=[pl.BlockSpec((t, d), lambda i, j: (i, 0)),
                  row_prev, row_cur, row_prev, row_cur, sel_spec, sel_spec,
                  pl.BlockSpec((e_blk, d), lambda i, j: (jnp.minimum(j, nj - 1), 0)),
                  pl.BlockSpec((d, e_blk), lambda i, j: (0, jnp.maximum(j - 1, 0))),
                  pl.BlockSpec((t, d), lambda i, j: (i, 0), pipeline_mode=pl.Buffered(1)),
                  pl.BlockSpec((1, d), lambda i, j: (0, 0))],
        out_specs=pl.BlockSpec((t, d), lambda i, j: (i, 0)),
        out_shape=jax.ShapeDtypeStruct((n, d), F32),
        scratch_shapes=[pltpu.VMEM((d, t), F32),
                        pltpu.VMEM((e_blk // 2, t), F32), pltpu.VMEM((e_blk // 2, t), F32),
                        pltpu.VMEM((e_blk // 2, t), BF16), pltpu.VMEM((e_blk // 2, t), BF16)],
        compiler_params=_cparams(("parallel", "arbitrary")),
        name="peer_dense",
    )(hx, th, th, e1, e1, s2, e2, u_tab, vt_tab, res, g_final.reshape(1, d).astype(F32))


def _peer_block(x, ya, yb, w_out, g_ffn, wq, k1, k2, u_tab, v_tab, g_final, final_norm):
    h, hx, th, e1, s2, e2 = _peer_select(x, ya, yb, w_out, g_ffn, wq, k1, k2)
    return _peer_dense(hx, th, e1, s2, e2, u_tab.astype(BF16), v_tab.astype(BF16).T, h, g_final,
                       final_norm)


def kernel(x, l0_norm_mix_g, l0_w_in, a_lam_re, a_lam_im, a_log_dt, a_b_re, a_b_im, a_c_re, a_c_im, a_d, a_glu_w, a_glu_b, b_conv_w, l0_w_out, l0_norm_ffn_g, l0_peer_wq, l0_peer_k1, l0_peer_k2, l0_peer_u, l0_peer_v, l1_norm_mix_g, l1_w_in, c_vnorm_g, c_ws, c_bs, l1_w_out, l1_norm_ffn_g, l1_peer_wq, l1_peer_k1, l1_peer_k2, l1_peer_u, l1_peer_v, final_norm_g):
    batch, seq, d = x.shape
    n = batch * seq
    h = x.reshape(n, d)

    (p0,) = _norm_matmul(h, l0_norm_mix_g.astype(F32), l0_w_in.astype(BF16),
                         ((0, l0_w_in.shape[1], 1.0),), (F32,))
    bq, are, aim, cq, apr, api = _s5_params(a_lam_re, a_lam_im, a_log_dt, a_b_re, a_b_im, a_c_re, a_c_im,
                                            min(T_S5, seq) // S5_SEGMENTS)
    ya, yb = _even_mixer(p0, batch, bq, are, aim, apr, api, cq, a_d, a_glu_w, a_glu_b, b_conv_w)
    h = _peer_block(h, ya, yb, l0_w_out, l0_norm_ffn_g, l0_peer_wq, l0_peer_k1, l0_peer_k2,
                    l0_peer_u, l0_peer_v, final_norm_g, False)

    cw = c_vnorm_g.shape[0]
    dw = (l1_w_in.shape[1] - 2 * cw) // 3
    o = 2 * cw
    splits = ((0, o, 1.0), (o, o + dw, SB_HEAD_DIM ** -0.5), (o + dw, o + 2 * dw, 1.0),
              (o + 2 * dw, o + 3 * dw, 1.0))
    uv, q, k, v = _norm_matmul(h, l1_norm_mix_g.astype(F32), l1_w_in.astype(BF16), splits,
                               (F32, BF16, BF16, BF16))
    yc = _gmlp(uv, c_vnorm_g, c_ws, c_bs)
    yd = _sb_attention(q, k, v, batch)
    h = _peer_block(h, yc, yd, l1_w_out, l1_norm_ffn_g, l1_peer_wq, l1_peer_k1, l1_peer_k2,
                    l1_peer_u, l1_peer_v, final_norm_g, True)
    return h.reshape(batch, seq, d)
```
